```python
import jax, jax.numpy as jnp
from jax import lax
import numpy as np

D_MODEL = 1024
BATCH = 4
SEQ = 4096
DEPTH = 2

GDN_HEADS = 4
GDN_DK = 128
GDN_DV = 128
GDN_CHUNK = 64
CONV_K = 4
MLA_HEADS = 8
MLA_Q_LORA = 384
MLA_KV_LORA = 256
MLA_NOPE = 128
MLA_ROPE = 64
MLA_DV = 128
ROPE_THETA = 10000.0
Q_BLOCK = 128
LRU_WIDTH = 512
LRU_BLOCKS = 8
LRU_BW = LRU_WIDTH // LRU_BLOCKS
RG_C = 8.0

RMS_EPS = 1e-6
L2_EPS = 1e-6

GDN_WIDTH = GDN_HEADS * GDN_DV
MLA_WIDTH = MLA_HEADS * MLA_DV
MIX_WIDTH = GDN_WIDTH + MLA_WIDTH + LRU_WIDTH
MLA_QK = MLA_NOPE + MLA_ROPE
IN_SPLITS = (GDN_HEADS * GDN_DK, GDN_HEADS * GDN_DK, GDN_WIDTH, GDN_HEADS, GDN_HEADS, GDN_WIDTH,
             MLA_Q_LORA, MLA_KV_LORA, MLA_ROPE, MLA_WIDTH,
             LRU_WIDTH, LRU_WIDTH)
IN_WIDTH = sum(IN_SPLITS)

kernel_name = "hybrid_gdn_mla_rglru_parallel_heads"


def rmsnorm(x, gain):
    xf = x.astype(jnp.float32)
    y = xf * lax.rsqrt(jnp.mean(xf * xf, axis=-1, keepdims=True) + RMS_EPS)
    return (y * gain.astype(jnp.float32)).astype(x.dtype)


def l2norm(x):
    return x * lax.rsqrt(jnp.sum(x * x, axis=-1, keepdims=True) + L2_EPS)


def causal_dwconv(x, w):
    K, C = w.shape
    return lax.conv_general_dilated(
        x, w[:, None, :].astype(x.dtype), window_strides=(1,), padding=[(K - 1, 0)],
        dimension_numbers=("NWC", "WIO", "NWC"), feature_group_count=C)


def gated_delta_rule(q, k, v, g, beta):
    B, S, H, DK = q.shape
    DV = v.shape[-1]
    C = GDN_CHUNK
    N = S // C

    def chunks(t):
        return t.reshape(B, N, C, H, t.shape[-1]).transpose(0, 1, 3, 2, 4)

    qc = chunks(q * DK ** -0.5)
    kc = chunks(k)
    vc = chunks(v)
    gc = jnp.cumsum(g.reshape(B, N, C, H).transpose(0, 1, 3, 2), axis=-1)
    bc = beta.reshape(B, N, C, H).transpose(0, 1, 3, 2)[..., None]
    tril = jnp.tril(jnp.ones((C, C), bool))
    strict = jnp.tril(jnp.ones((C, C), bool), -1)
    decay = jnp.exp(jnp.where(tril, gc[..., :, None] - gc[..., None, :], -jnp.inf))
    kbeta = kc * bc
    a_mat = jnp.where(strict, jnp.einsum('bnhid,bnhjd->bnhij', kbeta, kc) * decay, 0.0) \
        + jnp.eye(C, dtype=q.dtype)
    rhs = jnp.concatenate([vc * bc, kbeta * jnp.exp(gc)[..., None]], axis=-1)
    sol = lax.linalg.triangular_solve(a_mat, rhs, left_side=True, lower=True, unit_diagonal=True)
    u, w = sol[..., :DV], sol[..., DV:]
    attn = jnp.where(tril, jnp.einsum('bnhid,bnhjd->bnhij', qc, kc) * decay, 0.0)

    def step(state, inp):
        q_n, k_n, u_n, w_n, g_n, a_n = inp
        v_new = u_n - jnp.einsum('bhck,bhkv->bhcv', w_n, state)
        o = jnp.einsum('bhck,bhkv->bhcv', q_n * jnp.exp(g_n)[..., None], state) \
            + jnp.einsum('bhij,bhjv->bhiv', a_n, v_new)
        g_last = g_n[..., -1]
        state = state * jnp.exp(g_last)[..., None, None] + jnp.einsum(
            'bhck,bhcv->bhkv', k_n * jnp.exp(g_last[..., None] - g_n)[..., None], v_new)
        return state, o

    xs = (jnp.moveaxis(qc, 1, 0), jnp.moveaxis(kc, 1, 0), jnp.moveaxis(u, 1, 0),
          jnp.moveaxis(w, 1, 0), jnp.moveaxis(gc, 1, 0), jnp.moveaxis(attn, 1, 0))
    state0 = jnp.zeros((B, H, DK, DV), q.dtype)
    _, o = lax.scan(step, state0, xs)
    return o.transpose(1, 0, 3, 2, 4).reshape(B, S, H, DV)


def rope_tables(S):
    inv = 1.0 / (ROPE_THETA ** (jnp.arange(0, MLA_ROPE, 2, dtype=jnp.float32) / MLA_ROPE))
    ang = jnp.arange(S, dtype=jnp.float32)[:, None] * inv[None, :]
    ang = jnp.concatenate([ang, ang], axis=-1)
    return jnp.cos(ang)[None, :, None, :], jnp.sin(ang)[None, :, None, :]


def apply_rope(t, cos, sin):
    half = t.shape[-1] // 2
    rot = jnp.concatenate([-t[..., half:], t[..., :half]], axis=-1)
    return (t * cos + rot * sin).astype(t.dtype)


def causal_block_attention(q, k, v):
    B, H, S, D = q.shape
    nb = S // Q_BLOCK
    scale = D ** -0.5
    qb = q.reshape(B, H, nb, Q_BLOCK, D).transpose(2, 0, 1, 3, 4)
    kpos = jnp.arange(S)

    def one_block(args):
        qi, i = args
        s = jnp.einsum('bhqd,bhkd->bhqk', qi, k).astype(jnp.float32) * scale
        qpos = i * Q_BLOCK + jnp.arange(Q_BLOCK)
        s = jnp.where(kpos[None, :] <= qpos[:, None], s, -jnp.inf)
        p = jax.nn.softmax(s, axis=-1).astype(v.dtype)
        return jnp.einsum('bhqk,bhkd->bhqd', p, v)

    out = lax.map(one_block, (qb, jnp.arange(nb)))
    return out.transpose(1, 0, 3, 2, 4).reshape(B, S, H * v.shape[-1])


def rg_lru(x, w_r, b_r, w_i, b_i, a_param):
    B, S, W = x.shape
    xb = x.reshape(B, S, LRU_BLOCKS, LRU_BW)
    r = jax.nn.sigmoid(jnp.einsum('bsnc,ncd->bsnd', xb, w_r).reshape(B, S, W) + b_r)
    i = jax.nn.sigmoid(jnp.einsum('bsnc,ncd->bsnd', xb, w_i).reshape(B, S, W) + b_i)
    log_a = -RG_C * r * jax.nn.softplus(-a_param)
    a = jnp.exp(log_a)
    b = jnp.sqrt(-jnp.expm1(2.0 * log_a)) * (i * x)

    def combine(lhs, rhs):
        a1, b1 = lhs
        a2, b2 = rhs
        return a1 * a2, a2 * b1 + b2

    _, h = lax.associative_scan(combine, (a, b), axis=1)
    return h


def hybrid_layer(x, cos, sin, norm_gain, w_in, gdn_conv_w, gdn_a_log, gdn_dt_bias, gdn_out_norm,
                 mla_q_norm, mla_w_uq, mla_kv_norm, mla_w_ukv, lru_conv_w, lru_conv_b,
                 lru_w_r, lru_b_r, lru_w_i, lru_b_i, lru_a_param, w_out):
    B, S, _ = x.shape
    dt = x.dtype
    h = rmsnorm(x, norm_gain)
    proj = h @ w_in
    offsets = [int(o) for o in np.cumsum(IN_SPLITS)[:-1]]
    (gq, gk, gv, gb, ga, gz, mq, mkv, mkr, mz, lx, lz) = jnp.split(proj, offsets, axis=-1)

    qkv = jax.nn.silu(causal_dwconv(jnp.concatenate([gq, gk, gv], axis=-1), gdn_conv_w))
    qkv = qkv.astype(jnp.float32)
    q = l2norm(qkv[..., :GDN_HEADS * GDN_DK].reshape(B, S, GDN_HEADS, GDN_DK))
    k = l2norm(qkv[..., GDN_HEADS * GDN_DK:2 * GDN_HEADS * GDN_DK].reshape(B, S, GDN_HEADS, GDN_DK))
    v = qkv[..., 2 * GDN_HEADS * GDN_DK:].reshape(B, S, GDN_HEADS, GDN_DV)
    beta = jax.nn.sigmoid(gb.astype(jnp.float32))
    g = -jnp.exp(gdn_a_log.astype(jnp.float32)) * jax.nn.softplus(
        ga.astype(jnp.float32) + gdn_dt_bias.astype(jnp.float32))
    o_a = gated_delta_rule(q, k, v, g, beta)
    o_a = rmsnorm(o_a, gdn_out_norm).reshape(B, S, GDN_WIDTH).astype(dt) * jax.nn.silu(gz)

    qm = (rmsnorm(mq, mla_q_norm) @ mla_w_uq).reshape(B, S, MLA_HEADS, MLA_QK)
    kv = (rmsnorm(mkv, mla_kv_norm) @ mla_w_ukv).reshape(B, S, MLA_HEADS, MLA_NOPE + MLA_DV)
    q_pe = apply_rope(qm[..., MLA_NOPE:], cos, sin)
    k_pe = apply_rope(mkr[:, :, None, :], cos, sin)
    qf = jnp.concatenate([qm[..., :MLA_NOPE], q_pe], axis=-1)
    kf = jnp.concatenate([kv[..., :MLA_NOPE],
                          jnp.broadcast_to(k_pe, (B, S, MLA_HEADS, MLA_ROPE))], axis=-1)
    vm = kv[..., MLA_NOPE:]
    o_b = causal_block_attention(qf.transpose(0, 2, 1, 3), kf.transpose(0, 2, 1, 3),
                                 vm.transpose(0, 2, 1, 3)).astype(dt) * jax.nn.silu(mz)

    u = (causal_dwconv(lx, lru_conv_w) + lru_conv_b).astype(jnp.float32)
    o_c = rg_lru(u, lru_w_r.astype(jnp.float32), lru_b_r.astype(jnp.float32),
                 lru_w_i.astype(jnp.float32), lru_b_i.astype(jnp.float32),
                 lru_a_param.astype(jnp.float32)).astype(dt) * jax.nn.silu(lz)

    mixed = jnp.concatenate([o_a, o_b, o_c], axis=-1)
    return x + mixed @ w_out


def setup_inputs(seed: int = 0) -> dict:
    key = jax.random.key(seed)
    ks = jax.random.split(key, 24)
    L = DEPTH
    f32 = jnp.float32

    def nrm(k, shape, scale):
        return scale * jax.random.normal(k, shape, f32)

    def gain(k, shape):
        return 1.0 + 0.01 * jax.random.normal(k, shape, f32)

    x = jax.random.normal(ks[0], (BATCH, SEQ, D_MODEL), f32)
    norm_gain = gain(ks[1], (L, D_MODEL))
    w_in = nrm(ks[2], (L, D_MODEL, IN_WIDTH), D_MODEL ** -0.5)
    gdn_conv_w = nrm(ks[3], (L, CONV_K, 2 * GDN_HEADS * GDN_DK + GDN_WIDTH), CONV_K ** -0.5)
    gdn_a_log = jnp.log(jax.random.uniform(ks[4], (L, GDN_HEADS), f32, 1.0, 16.0))
    dt0 = jnp.exp(jax.random.uniform(ks[5], (L, GDN_HEADS), f32, np.log(1e-3), np.log(1e-1)))
    gdn_dt_bias = dt0 + jnp.log(-jnp.expm1(-dt0))
    gdn_out_norm = gain(ks[6], (L, GDN_DV))
    mla_q_norm = gain(ks[7], (L, MLA_Q_LORA))
    mla_w_uq = nrm(ks[8], (L, MLA_Q_LORA, MLA_HEADS * MLA_QK), MLA_Q_LORA ** -0.5)
    mla_kv_norm = gain(ks[9], (L, MLA_KV_LORA))
    mla_w_ukv = nrm(ks[10], (L, MLA_KV_LORA, MLA_HEADS * (MLA_NOPE + MLA_DV)), MLA_KV_LORA ** -0.5)
    lru_conv_w = nrm(ks[11], (L, CONV_K, LRU_WIDTH), CONV_K ** -0.5)
    lru_conv_b = nrm(ks[12], (L, LRU_WIDTH), 0.01)
    lru_w_r = nrm(ks[13], (L, LRU_BLOCKS, LRU_BW, LRU_BW), LRU_BW ** -0.5)
    lru_b_r = nrm(ks[14], (L, LRU_WIDTH), 0.01)
    lru_w_i = nrm(ks[15], (L, LRU_BLOCKS, LRU_BW, LRU_BW), LRU_BW ** -0.5)
    lru_b_i = nrm(ks[16], (L, LRU_WIDTH), 0.01)
    a_target = jax.random.uniform(ks[17], (L, LRU_WIDTH), f32, 0.9, 0.999)
    sp = -jnp.log(a_target) / RG_C
    lru_a_param = -jnp.log(jnp.expm1(sp))
    w_out = nrm(ks[18], (L, MIX_WIDTH, D_MODEL), MIX_WIDTH ** -0.5)
    final_norm = gain(ks[19], (D_MODEL,))
    return {"x": x, "norm_gain": norm_gain, "w_in": w_in, "gdn_conv_w": gdn_conv_w,
            "gdn_a_log": gdn_a_log, "gdn_dt_bias": gdn_dt_bias, "gdn_out_norm": gdn_out_norm,
            "mla_q_norm": mla_q_norm, "mla_w_uq": mla_w_uq, "mla_kv_norm": mla_kv_norm,
            "mla_w_ukv": mla_w_ukv, "lru_conv_w": lru_conv_w, "lru_conv_b": lru_conv_b,
            "lru_w_r": lru_w_r, "lru_b_r": lru_b_r, "lru_w_i": lru_w_i, "lru_b_i": lru_b_i,
            "lru_a_param": lru_a_param, "w_out": w_out, "final_norm": final_norm}


def reference(x, norm_gain, w_in, gdn_conv_w, gdn_a_log, gdn_dt_bias, gdn_out_norm,
              mla_q_norm, mla_w_uq, mla_kv_norm, mla_w_ukv, lru_conv_w, lru_conv_b,
              lru_w_r, lru_b_r, lru_w_i, lru_b_i, lru_a_param, w_out, final_norm):
    cos, sin = rope_tables(x.shape[1])
    h = x
    for l in range(DEPTH):
        h = hybrid_layer(h, cos, sin, norm_gain[l], w_in[l], gdn_conv_w[l], gdn_a_log[l],
                         gdn_dt_bias[l], gdn_out_norm[l], mla_q_norm[l], mla_w_uq[l],
                         mla_kv_norm[l], mla_w_ukv[l], lru_conv_w[l], lru_conv_b[l],
                         lru_w_r[l], lru_b_r[l], lru_w_i[l], lru_b_i[l], lru_a_param[l], w_out[l])
    return rmsnorm(h, final_norm)
```

```python
import functools

import jax
import jax.numpy as jnp
from jax import lax
from jax.experimental import pallas as pl
from jax.experimental.pallas import tpu as pltpu

F32 = jnp.float32
BF16 = jnp.bfloat16
HIGHEST = lax.Precision.HIGHEST

D_MODEL = 1024
GDN_HEADS = 4
GDN_DK = 128
GDN_DV = 128
GDN_CHUNK = 64
CONV_K = 4
MLA_HEADS = 8
MLA_Q_LORA = 384
MLA_KV_LORA = 256
MLA_NOPE = 128
MLA_ROPE = 64
MLA_DV = 128
MLA_QK = MLA_NOPE + MLA_ROPE
ROPE_THETA = 10000.0
LRU_WIDTH = 512
LRU_BLOCKS = 8
LRU_BW = LRU_WIDTH // LRU_BLOCKS
LRU_GROUP = 256
RG_C = 8.0
RMS_EPS = 1e-6
L2_EPS = 1e-6

GDN_WIDTH = GDN_HEADS * GDN_DV
GDN_QKV = 3 * GDN_WIDTH
MLA_WIDTH = MLA_HEADS * MLA_DV
MIX_WIDTH = GDN_WIDTH + MLA_WIDTH + LRU_WIDTH
CONV_PAD = 8

IN_SEGS = (("qkv", GDN_QKV, BF16), ("gz", GDN_WIDTH, BF16), ("mq", MLA_Q_LORA, BF16),
           ("mkv", MLA_KV_LORA, BF16), ("mz", MLA_WIDTH, BF16), ("lx", LRU_WIDTH, BF16),
           ("lz", LRU_WIDTH, BF16), ("pe", 128, F32), ("gt", 128, F32))
IN_TOTAL = sum(s[1] for s in IN_SEGS)

VMEM_LIMIT = 56 * 1024 * 1024


def _sigmoid(x):
    return 1.0 / (1.0 + jnp.exp(-x))


def _softplus(x):
    return jnp.maximum(x, 0.0) + jnp.log1p(jnp.exp(-jnp.abs(x)))


def _silu(x):
    return x * _sigmoid(x)


def _rms(x, gain):
    return x * lax.rsqrt(jnp.mean(x * x, axis=-1, keepdims=True) + RMS_EPS) * gain


def _dot(a, b):
    return jnp.dot(a, b, preferred_element_type=F32)


def _inproj_kernel(x_ref, g_ref, w_ref, *out_refs):
    h = _rms(x_ref[...], g_ref[...]).astype(BF16)
    off = 0
    for ref, (_, width, _) in zip(out_refs, IN_SEGS):
        ref[...] = _dot(h, w_ref[:, off:off + width]).astype(ref.dtype)
        off += width


def _inproj(x2d, gain, w_all, tm):
    t = x2d.shape[0]
    return pl.pallas_call(
        _inproj_kernel,
        grid=(t // tm,),
        in_specs=[pl.BlockSpec((tm, D_MODEL), lambda i: (i, 0)),
                  pl.BlockSpec((1, D_MODEL), lambda i: (0, 0)),
                  pl.BlockSpec((D_MODEL, IN_TOTAL), lambda i: (0, 0))],
        out_specs=[pl.BlockSpec((tm, w), lambda i: (i, 0)) for _, w, _ in IN_SEGS],
        out_shape=[jax.ShapeDtypeStruct((t, w), dt) for _, w, dt in IN_SEGS],
        compiler_params=pltpu.CompilerParams(dimension_semantics=("parallel",),
                                             vmem_limit_bytes=VMEM_LIMIT),
        name="inproj",
    )(x2d, gain, w_all)


def _gdn_kernel(qkv_ref, gt_ref, gz_ref, cw_ref, alog_ref, dtb_ref, gain_ref, o_ref, xbuf, state, *, bb):
    c = GDN_CHUNK

    @pl.when(pl.program_id(1) == 0)
    def _():
        xbuf[:, 0:CONV_PAD, :] = jnp.zeros((bb, CONV_PAD, GDN_QKV), F32)
        state[...] = jnp.zeros(state.shape, F32)

    xbuf[:, CONV_PAD:CONV_PAD + c, :] = qkv_ref[...].astype(F32)

    row = lax.broadcasted_iota(jnp.int32, (c, c), 0)
    col = lax.broadcasted_iota(jnp.int32, (c, c), 1)
    tril = row >= col
    strict = row > col
    tril_f = jnp.where(tril, 1.0, 0.0).astype(F32)
    eye = jnp.where(row == col, 1.0, 0.0).astype(F32)

    for b in range(bb):
        y = cw_ref[0:1, :] * xbuf[b, CONV_PAD - 3:CONV_PAD - 3 + c, :]
        for k in range(1, CONV_K):
            y = y + cw_ref[k:k + 1, :] * xbuf[b, CONV_PAD - 3 + k:CONV_PAD - 3 + k + c, :]
        y = _silu(y)

        gt = gt_ref[b]
        beta_all = _sigmoid(gt)
        g_all = -jnp.exp(alog_ref[...]) * _softplus(gt + dtb_ref[...])
        gc = jnp.dot(tril_f, g_all, precision=HIGHEST, preferred_element_type=F32)
        gc_t = gc.T

        for h in range(GDN_HEADS):
            q = y[:, h * GDN_DK:(h + 1) * GDN_DK]
            k = y[:, GDN_WIDTH + h * GDN_DK:GDN_WIDTH + (h + 1) * GDN_DK]
            v = y[:, 2 * GDN_WIDTH + h * GDN_DV:2 * GDN_WIDTH + (h + 1) * GDN_DV]
            q = q * lax.rsqrt(jnp.sum(q * q, axis=-1, keepdims=True) + L2_EPS) * (GDN_DK ** -0.5)
            k = k * lax.rsqrt(jnp.sum(k * k, axis=-1, keepdims=True) + L2_EPS)
            beta = beta_all[:, h:h + 1]
            g_col = gc[:, GDN_HEADS + h:GDN_HEADS + h + 1]
            g_row = gc_t[GDN_HEADS + h:GDN_HEADS + h + 1, :]
            g_last = gc[c - 1:c, GDN_HEADS + h:GDN_HEADS + h + 1]

            decay = jnp.where(tril, jnp.exp(jnp.where(tril, g_col - g_row, 0.0)), 0.0)
            k_t = k.T
            k_t16 = k_t.astype(BF16)
            kb = k * beta
            a_mat = jnp.where(strict, _dot(kb.astype(BF16), k_t16) * decay, 0.0)

            n_pow = -a_mat
            t_inv = eye + n_pow
            sq = 2
            while sq < c:
                n_pow = jnp.dot(n_pow, n_pow, precision=HIGHEST, preferred_element_type=F32)
                t_inv = t_inv + jnp.dot(t_inv, n_pow, precision=HIGHEST, preferred_element_type=F32)
                sq *= 2

            rhs = jnp.concatenate([v * beta, kb * jnp.exp(g_col)], axis=1)
            uw = _dot(t_inv.astype(BF16), rhs.astype(BF16))
            u = uw[:, :GDN_DV]
            w = uw[:, GDN_DV:]
            attn = jnp.where(tril, _dot(q.astype(BF16), k_t16) * decay, 0.0)

            s_old = state[b * GDN_HEADS + h]
            wq = jnp.concatenate([w, q * jnp.exp(g_col)], axis=0)
            wqs = _dot(wq.astype(BF16), s_old.astype(BF16))
            v_new = u - wqs[:c]
            o = wqs[c:] + _dot(attn.astype(BF16), v_new.astype(BF16))
            kd_t = k_t * jnp.exp(g_last - g_row)
            state[b * GDN_HEADS + h] = s_old * jnp.exp(g_last) + _dot(kd_t.astype(BF16), v_new.astype(BF16))

            gz = gz_ref[b, :, h * GDN_DV:(h + 1) * GDN_DV].astype(F32)
            o_ref[b, :, h * GDN_DV:(h + 1) * GDN_DV] = (_rms(o, gain_ref[...]) * _silu(gz)).astype(o_ref.dtype)

    xbuf[:, 0:CONV_PAD, :] = xbuf[:, c:c + CONV_PAD, :]


def _gdn(qkv, gt, gz, conv_w, alog_row, dtb_row, gain, bb):
    b, s, _ = qkv.shape
    c = GDN_CHUNK
    return pl.pallas_call(
        functools.partial(_gdn_kernel, bb=bb),
        grid=(b // bb, s // c),
        in_specs=[pl.BlockSpec((bb, c, GDN_QKV), lambda i, n: (i, n, 0)),
                  pl.BlockSpec((bb, c, 128), lambda i, n: (i, n, 0)),
                  pl.BlockSpec((bb, c, GDN_WIDTH), lambda i, n: (i, n, 0)),
                  pl.BlockSpec((CONV_K, GDN_QKV), lambda i, n: (0, 0)),
                  pl.BlockSpec((1, 128), lambda i, n: (0, 0)),
                  pl.BlockSpec((1, 128), lambda i, n: (0, 0)),
                  pl.BlockSpec((1, GDN_DV), lambda i, n: (0, 0))],
        out_specs=pl.BlockSpec((bb, c, GDN_WIDTH), lambda i, n: (i, n, 0)),
        out_shape=jax.ShapeDtypeStruct((b, s, GDN_WIDTH), BF16),
        scratch_shapes=[pltpu.VMEM((bb, c + CONV_PAD, GDN_QKV), F32),
                        pltpu.VMEM((bb * GDN_HEADS, GDN_DK, GDN_DV), F32)],
        compiler_params=pltpu.CompilerParams(dimension_semantics=("parallel", "arbitrary"),
                                             vmem_limit_bytes=VMEM_LIMIT),
        name="gdn",
    )(qkv, gt, gz, conv_w, alog_row, dtb_row, gain)


def _mla_prep_kernel(mq_ref, mkv_ref, pe_ref, cos_ref, sin_ref, qg_ref, kvg_ref, wq_ref, wkv_ref,
                     q_ref, k_ref, v_ref, *, qscale):
    qn = _rms(mq_ref[0].astype(F32), qg_ref[...]).astype(BF16)
    kvn = _rms(mkv_ref[0].astype(F32), kvg_ref[...]).astype(BF16)
    cos = cos_ref[...]
    sin = sin_ref[...]
    nope_w = MLA_HEADS * MLA_NOPE
    rope_w = MLA_HEADS * MLA_ROPE
    q_nope = _dot(qn, wq_ref[:, 0:nope_w])
    q_t = _dot(qn, wq_ref[:, nope_w:nope_w + rope_w])
    q_r = _dot(qn, wq_ref[:, nope_w + rope_w:nope_w + 2 * rope_w])
    for p in range(MLA_HEADS // 2):
        q_pe = q_t[:, p * 128:(p + 1) * 128] * cos + q_r[:, p * 128:(p + 1) * 128] * sin
        for j in range(2):
            h = 2 * p + j
            q_ref[0, h, :, 0:MLA_NOPE] = (q_nope[:, h * MLA_NOPE:(h + 1) * MLA_NOPE] * qscale).astype(q_ref.dtype)
            q_ref[0, h, :, MLA_NOPE:MLA_QK] = (q_pe[:, j * MLA_ROPE:(j + 1) * MLA_ROPE] * qscale).astype(q_ref.dtype)
    kv = _dot(kvn, wkv_ref[...])
    pe = pe_ref[0]
    k_pe = (pe[:, :MLA_ROPE] * cos[:, :MLA_ROPE] + pe[:, MLA_ROPE:] * sin[:, :MLA_ROPE]).astype(k_ref.dtype)
    hw = MLA_NOPE + MLA_DV
    for h in range(MLA_HEADS):
        k_ref[0, h, :, 0:MLA_NOPE] = kv[:, h * hw:h * hw + MLA_NOPE].astype(k_ref.dtype)
        k_ref[0, h, :, MLA_NOPE:MLA_QK] = k_pe
        v_ref[0, h] = kv[:, h * hw + MLA_NOPE:(h + 1) * hw].astype(v_ref.dtype)


def _mla_prep(mq, mkv, pe, cos2, sin2, qg, kvg, wq, wkv, ts, qscale):
    b, s, _ = mq.shape
    return pl.pallas_call(
        functools.partial(_mla_prep_kernel, qscale=qscale),
        grid=(b, s // ts),
        in_specs=[pl.BlockSpec((1, ts, MLA_Q_LORA), lambda i, j: (i, j, 0)),
                  pl.BlockSpec((1, ts, MLA_KV_LORA), lambda i, j: (i, j, 0)),
                  pl.BlockSpec((1, ts, 128), lambda i, j: (i, j, 0)),
                  pl.BlockSpec((ts, 128), lambda i, j: (j, 0)),
                  pl.BlockSpec((ts, 128), lambda i, j: (j, 0)),
                  pl.BlockSpec((1, MLA_Q_LORA), lambda i, j: (0, 0)),
                  pl.BlockSpec((1, MLA_KV_LORA), lambda i, j: (0, 0)),
                  pl.BlockSpec(wq.shape, lambda i, j: (0, 0)),
                  pl.BlockSpec(wkv.shape, lambda i, j: (0, 0))],
        out_specs=[pl.BlockSpec((1, MLA_HEADS, ts, MLA_QK), lambda i, j: (i, 0, j, 0)),
                   pl.BlockSpec((1, MLA_HEADS, ts, MLA_QK), lambda i, j: (i, 0, j, 0)),
                   pl.BlockSpec((1, MLA_HEADS, ts, MLA_DV), lambda i, j: (i, 0, j, 0))],
        out_shape=[jax.ShapeDtypeStruct((b, MLA_HEADS, s, MLA_QK), BF16),
                   jax.ShapeDtypeStruct((b, MLA_HEADS, s, MLA_QK), BF16),
                   jax.ShapeDtypeStruct((b, MLA_HEADS, s, MLA_DV), BF16)],
        compiler_params=pltpu.CompilerParams(dimension_semantics=("parallel", "parallel"),
                                             vmem_limit_bytes=VMEM_LIMIT),
        name="mla_prep",
    )(mq, mkv, pe, cos2, sin2, qg, kvg, wq, wkv)


def _attn_kernel(q_ref, k_ref, v_ref, mz_ref, o_ref, *, t):
    i = pl.program_id(2)
    q = q_ref[0, 0]

    def step(kb, carry, masked):
        m, l, acc = carry
        start = pl.multiple_of(kb * t, t)
        k_blk = k_ref[0, 0, pl.ds(start, t), :]
        v_blk = v_ref[0, 0, pl.ds(start, t), :]
        s = lax.dot_general(q, k_blk, (((1,), (1,)), ((), ())), preferred_element_type=F32)
        if masked:
            r = lax.broadcasted_iota(jnp.int32, (t, t), 0)
            c = lax.broadcasted_iota(jnp.int32, (t, t), 1)
            s = jnp.where(c <= r, s, -jnp.inf)
        m_new = jnp.maximum(m, jnp.max(s, axis=-1, keepdims=True))
        alpha = jnp.exp2(m - m_new)
        p = jnp.exp2(s - m_new)
        l = alpha * l + jnp.sum(p, axis=-1, keepdims=True)
        acc = alpha * acc + _dot(p.astype(BF16), v_blk)
        return m_new, l, acc

    init = (jnp.full((t, 1), -jnp.inf, F32), jnp.zeros((t, 1), F32), jnp.zeros((t, MLA_DV), F32))
    carry = lax.fori_loop(0, i, lambda kb, cr: step(kb, cr, False), init)
    _, l, acc = step(i, carry, True)
    z = mz_ref[0].astype(F32)
    o_ref[0] = (acc / l * _silu(z)).astype(o_ref.dtype)


def _attention(q, k, v, mz, t):
    b, h, s, _ = q.shape
    return pl.pallas_call(
        functools.partial(_attn_kernel, t=t),
        grid=(b, h, s // t),
        in_specs=[pl.BlockSpec((1, 1, t, MLA_QK), lambda bi, hi, i: (bi, hi, i, 0)),
                  pl.BlockSpec((1, 1, s, MLA_QK), lambda bi, hi, i: (bi, hi, 0, 0)),
                  pl.BlockSpec((1, 1, s, MLA_DV), lambda bi, hi, i: (bi, hi, 0, 0)),
                  pl.BlockSpec((1, t, MLA_DV), lambda bi, hi, i: (bi, i, hi))],
        out_specs=pl.BlockSpec((1, t, MLA_DV), lambda bi, hi, i: (bi, i, hi)),
        out_shape=jax.ShapeDtypeStruct((b, s, h * MLA_DV), BF16),
        compiler_params=pltpu.CompilerParams(dimension_semantics=("parallel", "parallel", "parallel"),
                                             vmem_limit_bytes=VMEM_LIMIT),
        name="attention",
    )(q, k, v, mz)


def _lru_kernel(lx_ref, lz_ref, cw_ref, cb_ref, wr_ref, br_ref, wi_ref, bi_ref, ap_ref, o_ref, xbuf, hc, *, ts):
    @pl.when(pl.program_id(1) == 0)
    def _():
        xbuf[0:CONV_PAD, :] = jnp.zeros((CONV_PAD, LRU_WIDTH), F32)
        hc[...] = jnp.zeros(hc.shape, F32)

    xbuf[CONV_PAD:CONV_PAD + ts, :] = lx_ref[0].astype(F32)
    u = cb_ref[...] + cw_ref[0:1, :] * xbuf[CONV_PAD - 3:CONV_PAD - 3 + ts, :]
    for k in range(1, CONV_K):
        u = u + cw_ref[k:k + 1, :] * xbuf[CONV_PAD - 3 + k:CONV_PAD - 3 + k + ts, :]
    u16 = u.astype(BF16)
    n_groups = LRU_WIDTH // LRU_GROUP
    r_lin = jnp.concatenate(
        [_dot(u16[:, g * LRU_GROUP:(g + 1) * LRU_GROUP], wr_ref[g]) for g in range(n_groups)], axis=1)
    i_lin = jnp.concatenate(
        [_dot(u16[:, g * LRU_GROUP:(g + 1) * LRU_GROUP], wi_ref[g]) for g in range(n_groups)], axis=1)
    r = _sigmoid(r_lin + br_ref[...])
    gate_i = _sigmoid(i_lin + bi_ref[...])
    log_a = -RG_C * r * _softplus(-ap_ref[...])
    a = jnp.exp(log_a)
    bco = jnp.sqrt(-jnp.tanh(log_a) * (a * a + 1.0)) * (gate_i * u)

    rows = lax.broadcasted_iota(jnp.int32, (ts, LRU_WIDTH), 0)
    d = 1
    while d < ts:
        keep = rows >= d
        a_s = jnp.where(keep, pltpu.roll(a, d, 0), 1.0)
        b_s = jnp.where(keep, pltpu.roll(bco, d, 0), 0.0)
        bco = a * b_s + bco
        a = a * a_s
        d *= 2
    h = a * hc[0:1, :] + bco
    hc[0:1, :] = h[ts - 1:ts, :]
    z = lz_ref[0].astype(F32)
    o_ref[0] = (h * _silu(z)).astype(o_ref.dtype)
    xbuf[0:CONV_PAD, :] = xbuf[ts:ts + CONV_PAD, :]


def _lru(lx, lz, conv_w, conv_b, wr, br, wi, bi, ap, ts):
    b, s, _ = lx.shape
    row = lambda i, j: (0, 0)
    return pl.pallas_call(
        functools.partial(_lru_kernel, ts=ts),
        grid=(b, s // ts),
        in_specs=[pl.BlockSpec((1, ts, LRU_WIDTH), lambda i, j: (i, j, 0)),
                  pl.BlockSpec((1, ts, LRU_WIDTH), lambda i, j: (i, j, 0)),
                  pl.BlockSpec((CONV_K, LRU_WIDTH), row),
                  pl.BlockSpec((1, LRU_WIDTH), row),
                  pl.BlockSpec(wr.shape, lambda i, j: (0, 0, 0)),
                  pl.BlockSpec((1, LRU_WIDTH), row),
                  pl.BlockSpec(wi.shape, lambda i, j: (0, 0, 0)),
                  pl.BlockSpec((1, LRU_WIDTH), row),
                  pl.BlockSpec((1, LRU_WIDTH), row)],
        out_specs=pl.BlockSpec((1, ts, LRU_WIDTH), lambda i, j: (i, j, 0)),
        out_shape=jax.ShapeDtypeStruct((b, s, LRU_WIDTH), BF16),
        scratch_shapes=[pltpu.VMEM((ts + CONV_PAD, LRU_WIDTH), F32),
                        pltpu.VMEM((8, LRU_WIDTH), F32)],
        compiler_params=pltpu.CompilerParams(dimension_semantics=("parallel", "arbitrary"),
                                             vmem_limit_bytes=VMEM_LIMIT),
        name="rglru",
    )(lx, lz, conv_w, conv_b, wr, br, wi, bi, ap)


def _outproj_kernel(x_ref, oa_ref, ob_ref, oc_ref, w_ref, g_ref, y_ref, *, final):
    a_end = GDN_WIDTH
    b_end = GDN_WIDTH + MLA_WIDTH
    y = (x_ref[...] + _dot(oa_ref[...], w_ref[0:a_end, :]) + _dot(ob_ref[...], w_ref[a_end:b_end, :])
         + _dot(oc_ref[...], w_ref[b_end:MIX_WIDTH, :]))
    if final:
        y = _rms(y, g_ref[...])
    y_ref[...] = y


def _outproj(x2d, oa, ob, oc, w_out, gain, tm, final):
    t = x2d.shape[0]
    return pl.pallas_call(
        functools.partial(_outproj_kernel, final=final),
        grid=(t // tm,),
        in_specs=[pl.BlockSpec((tm, D_MODEL), lambda i: (i, 0)),
                  pl.BlockSpec((tm, GDN_WIDTH), lambda i: (i, 0)),
                  pl.BlockSpec((tm, MLA_WIDTH), lambda i: (i, 0)),
                  pl.BlockSpec((tm, LRU_WIDTH), lambda i: (i, 0)),
                  pl.BlockSpec((MIX_WIDTH, D_MODEL), lambda i: (0, 0)),
                  pl.BlockSpec((1, D_MODEL), lambda i: (0, 0))],
        out_specs=pl.BlockSpec((tm, D_MODEL), lambda i: (i, 0)),
        out_shape=jax.ShapeDtypeStruct((t, D_MODEL), F32),
        compiler_params=pltpu.CompilerParams(dimension_semantics=("parallel",),
                                             vmem_limit_bytes=VMEM_LIMIT),
        name="outproj",
    )(x2d, oa, ob, oc, w_out, gain)


def _rot_half_cols(w):
    half = MLA_ROPE // 2
    return jnp.concatenate([-w[..., half:], w[..., :half]], axis=-1)


def _pack_w_in(w):
    o = [0]
    for width in (GDN_WIDTH, GDN_WIDTH, GDN_WIDTH, GDN_HEADS, GDN_HEADS, GDN_WIDTH,
                  MLA_Q_LORA, MLA_KV_LORA, MLA_ROPE, MLA_WIDTH, LRU_WIDTH, LRU_WIDTH):
        o.append(o[-1] + width)
    mkr = w[:, o[8]:o[9]]
    gates = jnp.concatenate([w[:, o[3]:o[5]], jnp.zeros((w.shape[0], 128 - 2 * GDN_HEADS), w.dtype)], axis=1)
    parts = [w[:, o[0]:o[3]], w[:, o[5]:o[6]], w[:, o[6]:o[7]], w[:, o[7]:o[8]], w[:, o[9]:o[10]],
             w[:, o[10]:o[11]], w[:, o[11]:o[12]], mkr, _rot_half_cols(mkr), gates]
    return jnp.concatenate(parts, axis=1).astype(BF16)


def _pack_w_uq(w):
    w3 = w.reshape(MLA_Q_LORA, MLA_HEADS, MLA_QK)
    nope = w3[:, :, :MLA_NOPE].reshape(MLA_Q_LORA, MLA_HEADS * MLA_NOPE)
    rope = w3[:, :, MLA_NOPE:]
    rot = _rot_half_cols(rope)
    return jnp.concatenate([nope, rope.reshape(MLA_Q_LORA, -1), rot.reshape(MLA_Q_LORA, -1)], axis=1).astype(BF16)


def _pack_block_diag(w):
    per = LRU_GROUP // LRU_BW
    groups = LRU_BLOCKS // per
    out = jnp.zeros((groups, per, LRU_BW, per, LRU_BW), w.dtype)
    w4 = w.reshape(groups, per, LRU_BW, LRU_BW)
    for j in range(per):
        out = out.at[:, j, :, j, :].set(w4[:, j])
    return out.reshape(groups, LRU_GROUP, LRU_GROUP).astype(BF16)


def _lane_row(vals, offset):
    return jnp.zeros((1, 128), F32).at[0, offset:offset + vals.shape[0]].set(vals.astype(F32))


def _rope_tables(s):
    inv = 1.0 / (ROPE_THETA ** (jnp.arange(0, MLA_ROPE, 2, dtype=F32) / MLA_ROPE))
    ang = jnp.arange(s, dtype=F32)[:, None] * inv[None, :]
    ang = jnp.concatenate([ang, ang, ang, ang], axis=-1)
    return jnp.cos(ang), jnp.sin(ang)


def kernel(x, norm_gain, w_in, gdn_conv_w, gdn_a_log, gdn_dt_bias, gdn_out_norm, mla_q_norm, mla_w_uq,
           mla_kv_norm, mla_w_ukv, lru_conv_w, lru_conv_b, lru_w_r, lru_b_r, lru_w_i, lru_b_i, lru_a_param,
           w_out, final_norm):
    b, s, d = x.shape
    depth = w_in.shape[0]
    t = b * s
    tm = min(512, t)
    ts = min(512, s)
    t_attn = min(256, s)
    ts_lru = min(256, s)
    bb = 1
    qscale = (MLA_QK ** -0.5) * 1.4426950408889634
    cos2, sin2 = _rope_tables(s)

    h = x.reshape(t, d)
    for l in range(depth):
        segs = _inproj(h, norm_gain[l].reshape(1, d), _pack_w_in(w_in[l]), tm)
        qkv, gz, mq, mkv, mz, lx, lz, pe, gt = [a.reshape(b, s, a.shape[-1]) for a in segs]

        o_a = _gdn(qkv, gt, gz, gdn_conv_w[l], _lane_row(gdn_a_log[l], GDN_HEADS),
                   _lane_row(gdn_dt_bias[l], GDN_HEADS), gdn_out_norm[l].reshape(1, GDN_DV), bb)

        q, k, v = _mla_prep(mq, mkv, pe, cos2, sin2, mla_q_norm[l].reshape(1, -1), mla_kv_norm[l].reshape(1, -1),
                            _pack_w_uq(mla_w_uq[l]), mla_w_ukv[l].astype(BF16), ts, qscale)
        o_b = _attention(q, k, v, mz, t_attn)

        o_c = _lru(lx, lz, lru_conv_w[l], lru_conv_b[l].reshape(1, -1), _pack_block_diag(lru_w_r[l]),
                   lru_b_r[l].reshape(1, -1), _pack_block_diag(lru_w_i[l]), lru_b_i[l].reshape(1, -1),
                   lru_a_param[l].reshape(1, -1), ts_lru)

        h = _outproj(h, o_a.reshape(t, -1), o_b.reshape(t, -1), o_c.reshape(t, -1), w_out[l].astype(BF16),
                     final_norm.reshape(1, d), tm, final=(l == depth - 1))
    return h.reshape(b, s, d)
```

```python
import functools

import jax
import jax.numpy as jnp
from jax import lax
from jax.experimental import pallas as pl
from jax.experimental.pallas import tpu as pltpu

F32 = jnp.float32
BF16 = jnp.bfloat16
HIGHEST = lax.Precision.HIGHEST

D_MODEL = 1024
GDN_HEADS = 4
GDN_DK = 128
GDN_DV = 128
GDN_CHUNK = 64
CONV_K = 4
MLA_HEADS = 8
MLA_Q_LORA = 384
MLA_KV_LORA = 256
MLA_NOPE = 128
MLA_ROPE = 64
MLA_DV = 128
MLA_QK = MLA_NOPE + MLA_ROPE
ROPE_THETA = 10000.0
LRU_WIDTH = 512
LRU_BLOCKS = 8
LRU_BW = LRU_WIDTH // LRU_BLOCKS
LRU_GROUP = 256
RG_C = 8.0
RMS_EPS = 1e-6
L2_EPS = 1e-6

GDN_WIDTH = GDN_HEADS * GDN_DV
GDN_QKV = 3 * GDN_WIDTH
MLA_WIDTH = MLA_HEADS * MLA_DV
MIX_WIDTH = GDN_WIDTH + MLA_WIDTH + LRU_WIDTH
CONV_PAD = 8

IN_SEGS = (("qkv", GDN_QKV, BF16), ("gz", GDN_WIDTH, BF16), ("mq", MLA_Q_LORA, BF16),
           ("mkv", MLA_KV_LORA, BF16), ("mz", MLA_WIDTH, BF16), ("lx", LRU_WIDTH, BF16),
           ("lz", LRU_WIDTH, BF16), ("pe", 128, F32), ("gt", 128, F32))
IN_TOTAL = sum(s[1] for s in IN_SEGS)

VMEM_LIMIT = 56 * 1024 * 1024


def _sigmoid(x):
    return 1.0 / (1.0 + jnp.exp(-x))


def _softplus(x):
    return jnp.maximum(x, 0.0) + jnp.log1p(jnp.exp(-jnp.abs(x)))


def _silu(x):
    return x * _sigmoid(x)


def _rms(x, gain):
    return x * lax.rsqrt(jnp.mean(x * x, axis=-1, keepdims=True) + RMS_EPS) * gain


def _dot(a, b):
    return jnp.dot(a, b, preferred_element_type=F32)


def _inproj_kernel(x_ref, g_ref, w_ref, *out_refs):
    h = _rms(x_ref[...], g_ref[...]).astype(BF16)
    off = 0
    for ref, (_, width, _) in zip(out_refs, IN_SEGS):
        ref[...] = _dot(h, w_ref[:, off:off + width]).astype(ref.dtype)
        off += width


def _inproj(x2d, gain, w_all, tm):
    t = x2d.shape[0]
    return pl.pallas_call(
        _inproj_kernel,
        grid=(t // tm,),
        in_specs=[pl.BlockSpec((tm, D_MODEL), lambda i: (i, 0)),
                  pl.BlockSpec((1, D_MODEL), lambda i: (0, 0)),
                  pl.BlockSpec((D_MODEL, IN_TOTAL), lambda i: (0, 0))],
        out_specs=[pl.BlockSpec((tm, w), lambda i: (i, 0)) for _, w, _ in IN_SEGS],
        out_shape=[jax.ShapeDtypeStruct((t, w), dt) for _, w, dt in IN_SEGS],
        compiler_params=pltpu.CompilerParams(dimension_semantics=("parallel",),
                                             vmem_limit_bytes=VMEM_LIMIT),
        name="inproj",
    )(x2d, gain, w_all)


def _gdn_kernel(qkv_ref, gt_ref, gz_ref, cw_ref, alog_ref, dtb_ref, gain_ref, o_ref, xbuf, state, *, bb):
    c = GDN_CHUNK

    @pl.when(pl.program_id(1) == 0)
    def _():
        xbuf[:, 0:CONV_PAD, :] = jnp.zeros((bb, CONV_PAD, GDN_QKV), F32)
        state[...] = jnp.zeros(state.shape, F32)

    xbuf[:, CONV_PAD:CONV_PAD + c, :] = qkv_ref[...].astype(F32)

    row = lax.broadcasted_iota(jnp.int32, (c, c), 0)
    col = lax.broadcasted_iota(jnp.int32, (c, c), 1)
    tril = row >= col
    strict = row > col
    tril_f = jnp.where(tril, 1.0, 0.0).astype(F32)
    eye = jnp.where(row == col, 1.0, 0.0).astype(F32)

    chains = [(b, h) for b in range(bb) for h in range(GDN_HEADS)]
    ys, gcs, gc_ts, betas = [], [], [], []
    for b in range(bb):
        y = cw_ref[0:1, :] * xbuf[b, CONV_PAD - 3:CONV_PAD - 3 + c, :]
        for k in range(1, CONV_K):
            y = y + cw_ref[k:k + 1, :] * xbuf[b, CONV_PAD - 3 + k:CONV_PAD - 3 + k + c, :]
        ys.append(_silu(y))
        gt = gt_ref[b]
        betas.append(_sigmoid(gt))
        g_all = -jnp.exp(alog_ref[...]) * _softplus(gt + dtb_ref[...])
        gc = jnp.dot(tril_f, g_all, precision=HIGHEST, preferred_element_type=F32)
        gcs.append(gc)
        gc_ts.append(gc.T)
    xbuf[:, 0:CONV_PAD, :] = xbuf[:, c:c + CONV_PAD, :]

    qs, k_ts, kbs, vbs, decays, g_cols, g_rows, g_lasts = [], [], [], [], [], [], [], []
    for b, h in chains:
        y = ys[b]
        q = y[:, h * GDN_DK:(h + 1) * GDN_DK]
        k = y[:, GDN_WIDTH + h * GDN_DK:GDN_WIDTH + (h + 1) * GDN_DK]
        v = y[:, 2 * GDN_WIDTH + h * GDN_DV:2 * GDN_WIDTH + (h + 1) * GDN_DV]
        q = q * lax.rsqrt(jnp.sum(q * q, axis=-1, keepdims=True) + L2_EPS) * (GDN_DK ** -0.5)
        k = k * lax.rsqrt(jnp.sum(k * k, axis=-1, keepdims=True) + L2_EPS)
        beta = betas[b][:, h:h + 1]
        g_col = gcs[b][:, GDN_HEADS + h:GDN_HEADS + h + 1]
        g_row = gc_ts[b][GDN_HEADS + h:GDN_HEADS + h + 1, :]
        qs.append(q)
        k_ts.append(k.T)
        kbs.append(k * beta)
        vbs.append(v * beta)
        decays.append(jnp.where(tril, jnp.exp(jnp.where(tril, g_col - g_row, 0.0)), 0.0))
        g_cols.append(g_col)
        g_rows.append(g_row)
        g_lasts.append(gcs[b][c - 1:c, GDN_HEADS + h:GDN_HEADS + h + 1])

    k_t16s = [k_t.astype(BF16) for k_t in k_ts]
    n_pows = [-jnp.where(strict, _dot(kb.astype(BF16), k_t16) * decay, 0.0)
              for kb, k_t16, decay in zip(kbs, k_t16s, decays)]
    attns = [jnp.where(tril, _dot(q.astype(BF16), k_t16) * decay, 0.0)
             for q, k_t16, decay in zip(qs, k_t16s, decays)]

    t_invs = [eye + n for n in n_pows]
    sq = 2
    while sq < c:
        n_pows = [_dot(n.astype(BF16), n.astype(BF16)) for n in n_pows]
        t_invs = [t_inv + _dot(t_inv.astype(BF16), n.astype(BF16)) for t_inv, n in zip(t_invs, n_pows)]
        sq *= 2

    uws = [_dot(t_inv.astype(BF16), jnp.concatenate([vb, kb * jnp.exp(g_col)], axis=1).astype(BF16))
           for t_inv, vb, kb, g_col in zip(t_invs, vbs, kbs, g_cols)]
    s_olds = [state[b * GDN_HEADS + h] for b, h in chains]
    wqss = [_dot(jnp.concatenate([uw[:, GDN_DV:], q * jnp.exp(g_col)], axis=0).astype(BF16), s_old.astype(BF16))
            for uw, q, g_col, s_old in zip(uws, qs, g_cols, s_olds)]
    v_news = [(uw[:, :GDN_DV] - wqs[:c]).astype(BF16) for uw, wqs in zip(uws, wqss)]
    outs = [wqs[c:] + _dot(attn.astype(BF16), v_new) for wqs, attn, v_new in zip(wqss, attns, v_news)]
    for (b, h), s_old, k_t, g_last, g_row, v_new in zip(chains, s_olds, k_ts, g_lasts, g_rows, v_news):
        kd_t = (k_t * jnp.exp(g_last - g_row)).astype(BF16)
        state[b * GDN_HEADS + h] = s_old * jnp.exp(g_last) + _dot(kd_t, v_new)
    for (b, h), o in zip(chains, outs):
        gz = gz_ref[b, :, h * GDN_DV:(h + 1) * GDN_DV].astype(F32)
        o_ref[b, :, h * GDN_DV:(h + 1) * GDN_DV] = (_rms(o, gain_ref[...]) * _silu(gz)).astype(o_ref.dtype)


def _gdn(qkv, gt, gz, conv_w, alog_row, dtb_row, gain, bb):
    b, s, _ = qkv.shape
    c = GDN_CHUNK
    return pl.pallas_call(
        functools.partial(_gdn_kernel, bb=bb),
        grid=(b // bb, s // c),
        in_specs=[pl.BlockSpec((bb, c, GDN_QKV), lambda i, n: (i, n, 0)),
                  pl.BlockSpec((bb, c, 128), lambda i, n: (i, n, 0)),
                  pl.BlockSpec((bb, c, GDN_WIDTH), lambda i, n: (i, n, 0)),
                  pl.BlockSpec((CONV_K, GDN_QKV), lambda i, n: (0, 0)),
                  pl.BlockSpec((1, 128), lambda i, n: (0, 0)),
                  pl.BlockSpec((1, 128), lambda i, n: (0, 0)),
                  pl.BlockSpec((1, GDN_DV), lambda i, n: (0, 0))],
        out_specs=pl.BlockSpec((bb, c, GDN_WIDTH), lambda i, n: (i, n, 0)),
        out_shape=jax.ShapeDtypeStruct((b, s, GDN_WIDTH), BF16),
        scratch_shapes=[pltpu.VMEM((bb, c + CONV_PAD, GDN_QKV), F32),
                        pltpu.VMEM((bb * GDN_HEADS, GDN_DK, GDN_DV), F32)],
        compiler_params=pltpu.CompilerParams(dimension_semantics=("parallel", "arbitrary"),
                                             vmem_limit_bytes=VMEM_LIMIT),
        name="gdn",
    )(qkv, gt, gz, conv_w, alog_row, dtb_row, gain)


def _mla_prep_kernel(mq_ref, mkv_ref, pe_ref, cos_ref, sin_ref, qg_ref, kvg_ref, wq_ref, wkv_ref,
                     q_ref, k_ref, v_ref, *, qscale):
    qn = _rms(mq_ref[0].astype(F32), qg_ref[...]).astype(BF16)
    kvn = _rms(mkv_ref[0].astype(F32), kvg_ref[...]).astype(BF16)
    cos = cos_ref[...]
    sin = sin_ref[...]
    nope_w = MLA_HEADS * MLA_NOPE
    rope_w = MLA_HEADS * MLA_ROPE
    q_nope = _dot(qn, wq_ref[:, 0:nope_w])
    q_t = _dot(qn, wq_ref[:, nope_w:nope_w + rope_w])
    q_r = _dot(qn, wq_ref[:, nope_w + rope_w:nope_w + 2 * rope_w])
    for p in range(MLA_HEADS // 2):
        q_pe = q_t[:, p * 128:(p + 1) * 128] * cos + q_r[:, p * 128:(p + 1) * 128] * sin
        for j in range(2):
            h = 2 * p + j
            q_ref[0, h, :, 0:MLA_NOPE] = (q_nope[:, h * MLA_NOPE:(h + 1) * MLA_NOPE] * qscale).astype(q_ref.dtype)
            q_ref[0, h, :, MLA_NOPE:MLA_QK] = (q_pe[:, j * MLA_ROPE:(j + 1) * MLA_ROPE] * qscale).astype(q_ref.dtype)
    kv = _dot(kvn, wkv_ref[...])
    pe = pe_ref[0]
    k_pe = (pe[:, :MLA_ROPE] * cos[:, :MLA_ROPE] + pe[:, MLA_ROPE:] * sin[:, :MLA_ROPE]).astype(k_ref.dtype)
    hw = MLA_NOPE + MLA_DV
    for h in range(MLA_HEADS):
        k_ref[0, h, :, 0:MLA_NOPE] = kv[:, h * hw:h * hw + MLA_NOPE].astype(k_ref.dtype)
        k_ref[0, h, :, MLA_NOPE:MLA_QK] = k_pe
        v_ref[0, h] = kv[:, h * hw + MLA_NOPE:(h + 1) * hw].astype(v_ref.dtype)


def _mla_prep(mq, mkv, pe, cos2, sin2, qg, kvg, wq, wkv, ts, qscale):
    b, s, _ = mq.shape
    return pl.pallas_call(
        functools.partial(_mla_prep_kernel, qscale=qscale),
        grid=(b, s // ts),
        in_specs=[pl.BlockSpec((1, ts, MLA_Q_LORA), lambda i, j: (i, j, 0)),
                  pl.BlockSpec((1, ts, MLA_KV_LORA), lambda i, j: (i, j, 0)),
                  pl.BlockSpec((1, ts, 128), lambda i, j: (i, j, 0)),
                  pl.BlockSpec((ts, 128), lambda i, j: (j, 0)),
                  pl.BlockSpec((ts, 128), lambda i, j: (j, 0)),
                  pl.BlockSpec((1, MLA_Q_LORA), lambda i, j: (0, 0)),
                  pl.BlockSpec((1, MLA_KV_LORA), lambda i, j: (0, 0)),
                  pl.BlockSpec(wq.shape, lambda i, j: (0, 0)),
                  pl.BlockSpec(wkv.shape, lambda i, j: (0, 0))],
        out_specs=[pl.BlockSpec((1, MLA_HEADS, ts, MLA_QK), lambda i, j: (i, 0, j, 0)),
                   pl.BlockSpec((1, MLA_HEADS, ts, MLA_QK), lambda i, j: (i, 0, j, 0)),
                   pl.BlockSpec((1, MLA_HEADS, ts, MLA_DV), lambda i, j: (i, 0, j, 0))],
        out_shape=[jax.ShapeDtypeStruct((b, MLA_HEADS, s, MLA_QK), BF16),
                   jax.ShapeDtypeStruct((b, MLA_HEADS, s, MLA_QK), BF16),
                   jax.ShapeDtypeStruct((b, MLA_HEADS, s, MLA_DV), BF16)],
        compiler_params=pltpu.CompilerParams(dimension_semantics=("parallel", "parallel"),
                                             vmem_limit_bytes=VMEM_LIMIT),
        name="mla_prep",
    )(mq, mkv, pe, cos2, sin2, qg, kvg, wq, wkv)


def _attn_kernel(q_ref, k_ref, v_ref, mz_ref, o_ref, s_a, s_b, p_a, p_b, m_ref, l_ref, alpha_ref, acc_ref, *, t):
    i = pl.program_id(2)
    q = q_ref[0, 0]
    s_bufs = (s_a, s_b)
    p_bufs = (p_a, p_b)
    lanes = m_ref.shape[1]

    def scores(kb):
        start = pl.multiple_of(kb * t, t)
        return lax.dot_general(q, k_ref[0, 0, pl.ds(start, t), :], (((1,), (1,)), ((), ())),
                               preferred_element_type=F32)

    def pv_stage(p_buf, kb):
        start = pl.multiple_of(kb * t, t)
        acc_ref[...] = alpha_ref[...] * acc_ref[...] + _dot(p_buf[...], v_ref[0, 0, pl.ds(start, t), :])

    def softmax_stage(s, p_buf):
        m_old = m_ref[...]
        m_new = jnp.maximum(m_old, jnp.max(s, axis=-1, keepdims=True))
        alpha = jnp.exp2(m_old - m_new)
        p = jnp.exp2(s - pltpu.repeat(m_new, t // lanes, axis=1))
        alpha_ref[...] = alpha
        l_ref[...] = alpha * l_ref[...] + jnp.sum(p, axis=-1, keepdims=True)
        m_ref[...] = m_new
        p_buf[...] = p.astype(p_buf.dtype)

    def step(kb, parity):
        pv_stage(p_bufs[1 - parity], jnp.maximum(kb - 1, 0))
        s_bufs[1 - parity][...] = scores(kb + 1)
        softmax_stage(s_bufs[parity][...], p_bufs[parity])

    def finish(parity):
        pv_stage(p_bufs[1 - parity], jnp.maximum(i - 1, 0))
        r = lax.broadcasted_iota(jnp.int32, (t, t), 0)
        c = lax.broadcasted_iota(jnp.int32, (t, t), 1)
        softmax_stage(jnp.where(c <= r, s_bufs[parity][...], -jnp.inf), p_bufs[parity])
        pv_stage(p_bufs[parity], i)
        z = mz_ref[0].astype(F32)
        o_ref[0] = (acc_ref[...] / l_ref[...] * _silu(z)).astype(o_ref.dtype)

    m_ref[...] = jnp.full(m_ref.shape, -jnp.inf, F32)
    l_ref[...] = jnp.zeros(l_ref.shape, F32)
    alpha_ref[...] = jnp.ones(alpha_ref.shape, F32)
    acc_ref[...] = jnp.zeros(acc_ref.shape, F32)
    p_b[...] = jnp.zeros(p_b.shape, p_b.dtype)
    s_a[...] = scores(0)

    def pair(k2, carry):
        step(2 * k2, 0)
        step(2 * k2 + 1, 1)
        return carry

    lax.fori_loop(0, lax.shift_right_logical(i, 1), pair, 0)

    @pl.when((i & 1) == 1)
    def _():
        step(i - 1, 0)
        finish(1)

    @pl.when((i & 1) == 0)
    def _():
        finish(0)


def _attention(q, k, v, mz, t):
    b, h, s, _ = q.shape
    return pl.pallas_call(
        functools.partial(_attn_kernel, t=t),
        grid=(b, h, s // t),
        in_specs=[pl.BlockSpec((1, 1, t, MLA_QK), lambda bi, hi, i: (bi, hi, i, 0)),
                  pl.BlockSpec((1, 1, s, MLA_QK), lambda bi, hi, i: (bi, hi, 0, 0)),
                  pl.BlockSpec((1, 1, s, MLA_DV), lambda bi, hi, i: (bi, hi, 0, 0)),
                  pl.BlockSpec((1, t, MLA_DV), lambda bi, hi, i: (bi, i, hi))],
        out_specs=pl.BlockSpec((1, t, MLA_DV), lambda bi, hi, i: (bi, i, hi)),
        out_shape=jax.ShapeDtypeStruct((b, s, h * MLA_DV), BF16),
        scratch_shapes=[pltpu.VMEM((t, t), F32), pltpu.VMEM((t, t), F32),
                        pltpu.VMEM((t, t), BF16), pltpu.VMEM((t, t), BF16),
                        pltpu.VMEM((t, 128), F32), pltpu.VMEM((t, 128), F32), pltpu.VMEM((t, 128), F32),
                        pltpu.VMEM((t, MLA_DV), F32)],
        compiler_params=pltpu.CompilerParams(dimension_semantics=("parallel", "parallel", "parallel"),
                                             vmem_limit_bytes=VMEM_LIMIT),
        name="attention",
    )(q, k, v, mz)


def _lru_kernel(lx_ref, lz_ref, cw_ref, cb_ref, wr_ref, br_ref, wi_ref, bi_ref, ap_ref, o_ref, xbuf, hc, *, ts):
    @pl.when(pl.program_id(1) == 0)
    def _():
        xbuf[0:CONV_PAD, :] = jnp.zeros((CONV_PAD, LRU_WIDTH), F32)
        hc[...] = jnp.zeros(hc.shape, F32)

    xbuf[CONV_PAD:CONV_PAD + ts, :] = lx_ref[0].astype(F32)
    u = cb_ref[...] + cw_ref[0:1, :] * xbuf[CONV_PAD - 3:CONV_PAD - 3 + ts, :]
    for k in range(1, CONV_K):
        u = u + cw_ref[k:k + 1, :] * xbuf[CONV_PAD - 3 + k:CONV_PAD - 3 + k + ts, :]
    u16 = u.astype(BF16)
    n_groups = LRU_WIDTH // LRU_GROUP
    r_lin = jnp.concatenate(
        [_dot(u16[:, g * LRU_GROUP:(g + 1) * LRU_GROUP], wr_ref[g]) for g in range(n_groups)], axis=1)
    i_lin = jnp.concatenate(
        [_dot(u16[:, g * LRU_GROUP:(g + 1) * LRU_GROUP], wi_ref[g]) for g in range(n_groups)], axis=1)
    r = _sigmoid(r_lin + br_ref[...])
    gate_i = _sigmoid(i_lin + bi_ref[...])
    log_a = -RG_C * r * _softplus(-ap_ref[...])
    a = jnp.exp(log_a)
    bco = jnp.sqrt(-jnp.tanh(log_a) * (a * a + 1.0)) * (gate_i * u)

    rows = lax.broadcasted_iota(jnp.int32, (ts, LRU_WIDTH), 0)
    d = 1
    while d < ts:
        keep = rows >= d
        a_s = jnp.where(keep, pltpu.roll(a, d, 0), 1.0)
        b_s = jnp.where(keep, pltpu.roll(bco, d, 0), 0.0)
        bco = a * b_s + bco
        a = a * a_s
        d *= 2
    h = a * hc[0:1, :] + bco
    hc[0:1, :] = h[ts - 1:ts, :]
    z = lz_ref[0].astype(F32)
    o_ref[0] = (h * _silu(z)).astype(o_ref.dtype)
    xbuf[0:CONV_PAD, :] = xbuf[ts:ts + CONV_PAD, :]


def _lru(lx, lz, conv_w, conv_b, wr, br, wi, bi, ap, ts):
    b, s, _ = lx.shape
    row = lambda i, j: (0, 0)
    return pl.pallas_call(
        functools.partial(_lru_kernel, ts=ts),
        grid=(b, s // ts),
        in_specs=[pl.BlockSpec((1, ts, LRU_WIDTH), lambda i, j: (i, j, 0)),
                  pl.BlockSpec((1, ts, LRU_WIDTH), lambda i, j: (i, j, 0)),
                  pl.BlockSpec((CONV_K, LRU_WIDTH), row),
                  pl.BlockSpec((1, LRU_WIDTH), row),
                  pl.BlockSpec(wr.shape, lambda i, j: (0, 0, 0)),
                  pl.BlockSpec((1, LRU_WIDTH), row),
                  pl.BlockSpec(wi.shape, lambda i, j: (0, 0, 0)),
                  pl.BlockSpec((1, LRU_WIDTH), row),
                  pl.BlockSpec((1, LRU_WIDTH), row)],
        out_specs=pl.BlockSpec((1, ts, LRU_WIDTH), lambda i, j: (i, j, 0)),
        out_shape=jax.ShapeDtypeStruct((b, s, LRU_WIDTH), BF16),
        scratch_shapes=[pltpu.VMEM((ts + CONV_PAD, LRU_WIDTH), F32),
                        pltpu.VMEM((8, LRU_WIDTH), F32)],
        compiler_params=pltpu.CompilerParams(dimension_semantics=("parallel", "arbitrary"),
                                             vmem_limit_bytes=VMEM_LIMIT),
        name="rglru",
    )(lx, lz, conv_w, conv_b, wr, br, wi, bi, ap)


def _outproj_kernel(x_ref, oa_ref, ob_ref, oc_ref, w_ref, g_ref, y_ref, *, final):
    a_end = GDN_WIDTH
    b_end = GDN_WIDTH + MLA_WIDTH
    y = (x_ref[...] + _dot(oa_ref[...], w_ref[0:a_end, :]) + _dot(ob_ref[...], w_ref[a_end:b_end, :])
         + _dot(oc_ref[...], w_ref[b_end:MIX_WIDTH, :]))
    if final:
        y = _rms(y, g_ref[...])
    y_ref[...] = y


def _outproj(x2d, oa, ob, oc, w_out, gain, tm, final):
    t = x2d.shape[0]
    return pl.pallas_call(
        functools.partial(_outproj_kernel, final=final),
        grid=(t // tm,),
        in_specs=[pl.BlockSpec((tm, D_MODEL), lambda i: (i, 0)),
                  pl.BlockSpec((tm, GDN_WIDTH), lambda i: (i, 0)),
                  pl.BlockSpec((tm, MLA_WIDTH), lambda i: (i, 0)),
                  pl.BlockSpec((tm, LRU_WIDTH), lambda i: (i, 0)),
                  pl.BlockSpec((MIX_WIDTH, D_MODEL), lambda i: (0, 0)),
                  pl.BlockSpec((1, D_MODEL), lambda i: (0, 0))],
        out_specs=pl.BlockSpec((tm, D_MODEL), lambda i: (i, 0)),
        out_shape=jax.ShapeDtypeStruct((t, D_MODEL), F32),
        compiler_params=pltpu.CompilerParams(dimension_semantics=("parallel",),
                                             vmem_limit_bytes=VMEM_LIMIT),
        name="outproj",
    )(x2d, oa, ob, oc, w_out, gain)


def _rot_half_cols(w):
    half = MLA_ROPE // 2
    return jnp.concatenate([-w[..., half:], w[..., :half]], axis=-1)


def _pack_w_in(w):
    o = [0]
    for width in (GDN_WIDTH, GDN_WIDTH, GDN_WIDTH, GDN_HEADS, GDN_HEADS, GDN_WIDTH,
                  MLA_Q_LORA, MLA_KV_LORA, MLA_ROPE, MLA_WIDTH, LRU_WIDTH, LRU_WIDTH):
        o.append(o[-1] + width)
    mkr = w[:, o[8]:o[9]]
    gates = jnp.concatenate([w[:, o[3]:o[5]], jnp.zeros((w.shape[0], 128 - 2 * GDN_HEADS), w.dtype)], axis=1)
    parts = [w[:, o[0]:o[3]], w[:, o[5]:o[6]], w[:, o[6]:o[7]], w[:, o[7]:o[8]], w[:, o[9]:o[10]],
             w[:, o[10]:o[11]], w[:, o[11]:o[12]], mkr, _rot_half_cols(mkr), gates]
    return jnp.concatenate(parts, axis=1).astype(BF16)


def _pack_w_uq(w):
    w3 = w.reshape(MLA_Q_LORA, MLA_HEADS, MLA_QK)
    nope = w3[:, :, :MLA_NOPE].reshape(MLA_Q_LORA, MLA_HEADS * MLA_NOPE)
    rope = w3[:, :, MLA_NOPE:]
    rot = _rot_half_cols(rope)
    return jnp.concatenate([nope, rope.reshape(MLA_Q_LORA, -1), rot.reshape(MLA_Q_LORA, -1)], axis=1).astype(BF16)


def _pack_block_diag(w):
    per = LRU_GROUP // LRU_BW
    groups = LRU_BLOCKS // per
    out = jnp.zeros((groups, per, LRU_BW, per, LRU_BW), w.dtype)
    w4 = w.reshape(groups, per, LRU_BW, LRU_BW)
    for j in range(per):
        out = out.at[:, j, :, j, :].set(w4[:, j])
    return out.reshape(groups, LRU_GROUP, LRU_GROUP).astype(BF16)


def _lane_row(vals, offset):
    return jnp.zeros((1, 128), F32).at[0, offset:offset + vals.shape[0]].set(vals.astype(F32))


def _rope_tables(s):
    inv = 1.0 / (ROPE_THETA ** (jnp.arange(0, MLA_ROPE, 2, dtype=F32) / MLA_ROPE))
    ang = jnp.arange(s, dtype=F32)[:, None] * inv[None, :]
    ang = jnp.concatenate([ang, ang, ang, ang], axis=-1)
    return jnp.cos(ang), jnp.sin(ang)


def kernel(x, norm_gain, w_in, gdn_conv_w, gdn_a_log, gdn_dt_bias, gdn_out_norm, mla_q_norm, mla_w_uq,
           mla_kv_norm, mla_w_ukv, lru_conv_w, lru_conv_b, lru_w_r, lru_b_r, lru_w_i, lru_b_i, lru_a_param,
           w_out, final_norm):
    b, s, d = x.shape
    depth = w_in.shape[0]
    t = b * s
    tm = min(512, t)
    ts = min(512, s)
    t_attn = min(512, s)
    ts_lru = min(256, s)
    bb = min(4, b)
    qscale = (MLA_QK ** -0.5) * 1.4426950408889634
    cos2, sin2 = _rope_tables(s)

    h = x.reshape(t, d)
    for l in range(depth):
        segs = _inproj(h, norm_gain[l].reshape(1, d), _pack_w_in(w_in[l]), tm)
        qkv, gz, mq, mkv, mz, lx, lz, pe, gt = [a.reshape(b, s, a.shape[-1]) for a in segs]

        o_a = _gdn(qkv, gt, gz, gdn_conv_w[l], _lane_row(gdn_a_log[l], GDN_HEADS),
                   _lane_row(gdn_dt_bias[l], GDN_HEADS), gdn_out_norm[l].reshape(1, GDN_DV), bb)

        q, k, v = _mla_prep(mq, mkv, pe, cos2, sin2, mla_q_norm[l].reshape(1, -1), mla_kv_norm[l].reshape(1, -1),
                            _pack_w_uq(mla_w_uq[l]), mla_w_ukv[l].astype(BF16), ts, qscale)
        o_b = _attention(q, k, v, mz, t_attn)

        o_c = _lru(lx, lz, lru_conv_w[l], lru_conv_b[l].reshape(1, -1), _pack_block_diag(lru_w_r[l]),
                   lru_b_r[l].reshape(1, -1), _pack_block_diag(lru_w_i[l]), lru_b_i[l].reshape(1, -1),
                   lru_a_param[l].reshape(1, -1), ts_lru)

        h = _outproj(h, o_a.reshape(t, -1), o_b.reshape(t, -1), o_c.reshape(t, -1), w_out[l].astype(BF16),
                     final_norm.reshape(1, d), tm, final=(l == depth - 1))
    return h.reshape(b, s, d)
```

```python
import functools

import jax
import jax.numpy as jnp
import numpy as np
from jax import lax
from jax.experimental import pallas as pl
from jax.experimental.pallas import tpu as pltpu

F32 = jnp.float32
BF16 = jnp.bfloat16
HIGHEST = lax.Precision.HIGHEST

D_MODEL = 1024
GDN_HEADS = 4
GDN_DK = 128
GDN_DV = 128
GDN_CHUNK = 64
CONV_K = 4
MLA_HEADS = 8
MLA_Q_LORA = 384
MLA_KV_LORA = 256
MLA_NOPE = 128
MLA_ROPE = 64
MLA_DV = 128
MLA_QK = MLA_NOPE + MLA_ROPE
ROPE_THETA = 10000.0
LRU_WIDTH = 512
LRU_BLOCKS = 8
LRU_BW = LRU_WIDTH // LRU_BLOCKS
LRU_GROUP = 256
RG_C = 8.0
RMS_EPS = 1e-6
L2_EPS = 1e-6

GDN_WIDTH = GDN_HEADS * GDN_DV
GDN_QKV = 3 * GDN_WIDTH
MLA_WIDTH = MLA_HEADS * MLA_DV
MIX_WIDTH = GDN_WIDTH + MLA_WIDTH + LRU_WIDTH
CONV_PAD = 8
CONV_COLS = GDN_QKV + LRU_WIDTH
CONV_GROUP = 256

IN_SEGS = (("qkv", GDN_QKV, BF16), ("lx", LRU_WIDTH, BF16), ("gz", GDN_WIDTH, BF16), ("mq", MLA_Q_LORA, BF16),
           ("mkv", MLA_KV_LORA, BF16), ("mz", MLA_WIDTH, BF16), ("lz", LRU_WIDTH, BF16),
           ("pe", 128, F32), ("gt", 128, F32))
IN_TOTAL = sum(s[1] for s in IN_SEGS)

VMEM_LIMIT = 56 * 1024 * 1024


def _sigmoid(x):
    return 1.0 / (1.0 + jnp.exp(-x))


def _softplus(x):
    return jnp.maximum(x, 0.0) + jnp.log1p(jnp.exp(-jnp.abs(x)))


def _silu(x):
    return x * _sigmoid(x)


def _rms(x, gain):
    return x * lax.rsqrt(jnp.mean(x * x, axis=-1, keepdims=True) + RMS_EPS) * gain


def _dot(a, b):
    return jnp.dot(a, b, preferred_element_type=F32)


def _inproj_kernel(x_ref, g_ref, w_ref, cw_ref, cb_ref, *refs, tiles_per_seq):
    out_refs, tail = refs[:-1], refs[-1]
    qkv_ref, lx_ref = out_refs[0], out_refs[1]
    tm = x_ref.shape[0]

    @pl.when(pl.program_id(0) % tiles_per_seq == 0)
    def _():
        tail[...] = jnp.zeros(tail.shape, F32)

    h = _rms(x_ref[...], g_ref[...]).astype(BF16)

    def conv_epilogue(lo, res):
        cols = slice(lo, lo + CONV_GROUP)
        ext = jnp.concatenate([tail[:, cols], res], axis=0)
        tail[:, cols] = res[tm - CONV_PAD:, :]
        first = CONV_PAD - (CONV_K - 1)
        y = cw_ref[0:1, cols] * ext[first:first + tm, :]
        for k in range(1, CONV_K):
            y = y + cw_ref[k:k + 1, cols] * ext[first + k:first + k + tm, :]
        if lo >= GDN_QKV:
            lx_ref[:, lo - GDN_QKV:lo - GDN_QKV + CONV_GROUP] = (
                y + cb_ref[:, lo - GDN_QKV:lo - GDN_QKV + CONV_GROUP]).astype(lx_ref.dtype)
            return
        y = _silu(y)
        if lo < 2 * GDN_WIDTH:
            heads = []
            for j in range(CONV_GROUP // GDN_DK):
                yh = y[:, j * GDN_DK:(j + 1) * GDN_DK]
                yh = yh * lax.rsqrt(jnp.sum(yh * yh, axis=-1, keepdims=True) + L2_EPS)
                heads.append(yh * (GDN_DK ** -0.5) if lo < GDN_WIDTH else yh)
            y = jnp.concatenate(heads, axis=1)
        qkv_ref[:, cols] = y.astype(qkv_ref.dtype)

    plain = []
    off = CONV_COLS
    for ref, (_, width, _) in zip(out_refs[2:], IN_SEGS[2:]):
        for sub in range(0, width, 2 * CONV_GROUP):
            plain.append((ref, sub, min(2 * CONV_GROUP, width - sub), off + sub))
        off += width
    conv_los = list(range(0, CONV_COLS, CONV_GROUP))
    pending = None
    for n in range(max(len(conv_los), len(plain))):
        if n < len(conv_los):
            res = _dot(h, w_ref[:, conv_los[n]:conv_los[n] + CONV_GROUP])
        if n < len(plain):
            ref, sub, width, col = plain[n]
            ref[:, sub:sub + width] = _dot(h, w_ref[:, col:col + width]).astype(ref.dtype)
        if pending is not None:
            conv_epilogue(*pending)
        pending = (conv_los[n], res) if n < len(conv_los) else None
    if pending is not None:
        conv_epilogue(*pending)


def _inproj(x2d, gain, w_all, conv_w, conv_b, tm, tiles_per_seq):
    t = x2d.shape[0]
    return pl.pallas_call(
        functools.partial(_inproj_kernel, tiles_per_seq=tiles_per_seq),
        grid=(t // tm,),
        in_specs=[pl.BlockSpec((tm, D_MODEL), lambda i: (i, 0)),
                  pl.BlockSpec((1, D_MODEL), lambda i: (0, 0)),
                  pl.BlockSpec((D_MODEL, IN_TOTAL), lambda i: (0, 0)),
                  pl.BlockSpec((CONV_K, CONV_COLS), lambda i: (0, 0)),
                  pl.BlockSpec((1, LRU_WIDTH), lambda i: (0, 0))],
        out_specs=[pl.BlockSpec((tm, w), lambda i: (i, 0)) for _, w, _ in IN_SEGS],
        out_shape=[jax.ShapeDtypeStruct((t, w), dt) for _, w, dt in IN_SEGS],
        scratch_shapes=[pltpu.VMEM((CONV_PAD, CONV_COLS), F32)],
        compiler_params=pltpu.CompilerParams(dimension_semantics=("arbitrary",),
                                             vmem_limit_bytes=VMEM_LIMIT),
        name="inproj",
    )(x2d, gain, w_all, conv_w, conv_b)


def _gdn_kernel(qkv_ref, gt_ref, gz_ref, alog_ref, dtb_ref, gain_ref, o_ref, state, *, bb):
    c = GDN_CHUNK

    @pl.when(pl.program_id(1) == 0)
    def _():
        state[...] = jnp.zeros(state.shape, F32)

    row = lax.broadcasted_iota(jnp.int32, (c, c), 0)
    col = lax.broadcasted_iota(jnp.int32, (c, c), 1)
    tril = row >= col
    strict = row > col
    tril_f = jnp.where(tril, 1.0, 0.0).astype(F32)
    eye = jnp.where(row == col, 1.0, 0.0).astype(F32)

    chains = [(b, h) for b in range(bb) for h in range(GDN_HEADS)]
    gcs, gc_ts, betas = [], [], []
    for b in range(bb):
        gt = gt_ref[b]
        betas.append(_sigmoid(gt))
        g_all = -jnp.exp(alog_ref[...]) * _softplus(gt + dtb_ref[...])
        gc = jnp.dot(tril_f, g_all, precision=HIGHEST, preferred_element_type=F32)
        gcs.append(gc)
        gc_ts.append(gc.T)

    qs, k_ts, kbs, vbs, decays, g_cols, g_rows, g_lasts = [], [], [], [], [], [], [], []
    for b, h in chains:
        q = qkv_ref[b, :, h * GDN_DK:(h + 1) * GDN_DK].astype(F32)
        k = qkv_ref[b, :, GDN_WIDTH + h * GDN_DK:GDN_WIDTH + (h + 1) * GDN_DK].astype(F32)
        v = qkv_ref[b, :, 2 * GDN_WIDTH + h * GDN_DV:2 * GDN_WIDTH + (h + 1) * GDN_DV].astype(F32)
        beta = betas[b][:, h:h + 1]
        g_col = gcs[b][:, GDN_HEADS + h:GDN_HEADS + h + 1]
        g_row = gc_ts[b][GDN_HEADS + h:GDN_HEADS + h + 1, :]
        qs.append(q)
        k_ts.append(k.T)
        kbs.append(k * beta)
        vbs.append(v * beta)
        decays.append(jnp.where(tril, jnp.exp(jnp.where(tril, g_col - g_row, 0.0)), 0.0))
        g_cols.append(g_col)
        g_rows.append(g_row)
        g_lasts.append(gcs[b][c - 1:c, GDN_HEADS + h:GDN_HEADS + h + 1])

    k_t16s = [k_t.astype(BF16) for k_t in k_ts]
    n_pows = [-jnp.where(strict, _dot(kb.astype(BF16), k_t16) * decay, 0.0)
              for kb, k_t16, decay in zip(kbs, k_t16s, decays)]
    attns = [jnp.where(tril, _dot(q.astype(BF16), k_t16) * decay, 0.0)
             for q, k_t16, decay in zip(qs, k_t16s, decays)]

    t_invs = [eye + n for n in n_pows]
    sq = 2
    while sq < c:
        n_pows = [_dot(n.astype(BF16), n.astype(BF16)) for n in n_pows]
        t_invs = [t_inv + _dot(t_inv.astype(BF16), n.astype(BF16)) for t_inv, n in zip(t_invs, n_pows)]
        sq *= 2

    uws = [_dot(t_inv.astype(BF16), jnp.concatenate([vb, kb * jnp.exp(g_col)], axis=1).astype(BF16))
           for t_inv, vb, kb, g_col in zip(t_invs, vbs, kbs, g_cols)]
    s_olds = [state[b * GDN_HEADS + h] for b, h in chains]
    wqss = [_dot(jnp.concatenate([uw[:, GDN_DV:], q * jnp.exp(g_col)], axis=0).astype(BF16), s_old.astype(BF16))
            for uw, q, g_col, s_old in zip(uws, qs, g_cols, s_olds)]
    v_news = [(uw[:, :GDN_DV] - wqs[:c]).astype(BF16) for uw, wqs in zip(uws, wqss)]
    outs = [wqs[c:] + _dot(attn.astype(BF16), v_new) for wqs, attn, v_new in zip(wqss, attns, v_news)]
    for (b, h), s_old, k_t, g_last, g_row, v_new in zip(chains, s_olds, k_ts, g_lasts, g_rows, v_news):
        kd_t = (k_t * jnp.exp(g_last - g_row)).astype(BF16)
        state[b * GDN_HEADS + h] = s_old * jnp.exp(g_last) + _dot(kd_t, v_new)
    for (b, h), o in zip(chains, outs):
        gz = gz_ref[b, :, h * GDN_DV:(h + 1) * GDN_DV].astype(F32)
        o_ref[b, :, h * GDN_DV:(h + 1) * GDN_DV] = (_rms(o, gain_ref[...]) * _silu(gz)).astype(o_ref.dtype)


def _gdn(qkv, gt, gz, alog_row, dtb_row, gain, bb):
    b, s, _ = qkv.shape
    c = GDN_CHUNK
    return pl.pallas_call(
        functools.partial(_gdn_kernel, bb=bb),
        grid=(b // bb, s // c),
        in_specs=[pl.BlockSpec((bb, c, GDN_QKV), lambda i, n: (i, n, 0)),
                  pl.BlockSpec((bb, c, 128), lambda i, n: (i, n, 0)),
                  pl.BlockSpec((bb, c, GDN_WIDTH), lambda i, n: (i, n, 0)),
                  pl.BlockSpec((1, 128), lambda i, n: (0, 0)),
                  pl.BlockSpec((1, 128), lambda i, n: (0, 0)),
                  pl.BlockSpec((1, GDN_DV), lambda i, n: (0, 0))],
        out_specs=pl.BlockSpec((bb, c, GDN_WIDTH), lambda i, n: (i, n, 0)),
        out_shape=jax.ShapeDtypeStruct((b, s, GDN_WIDTH), BF16),
        scratch_shapes=[pltpu.VMEM((bb * GDN_HEADS, GDN_DK, GDN_DV), F32)],
        compiler_params=pltpu.CompilerParams(dimension_semantics=("parallel", "arbitrary"),
                                             vmem_limit_bytes=VMEM_LIMIT),
        name="gdn",
    )(qkv, gt, gz, alog_row, dtb_row, gain)


def _mla_prep_kernel(mq_ref, mkv_ref, pe_ref, cos_ref, sin_ref, qg_ref, kvg_ref, wq_ref, wkv_ref,
                     q_ref, k_ref, v_ref, *, qscale):
    qn = _rms(mq_ref[0].astype(F32), qg_ref[...]).astype(BF16)
    kvn = _rms(mkv_ref[0].astype(F32), kvg_ref[...]).astype(BF16)
    cos = cos_ref[...]
    sin = sin_ref[...]
    nope_w = MLA_HEADS * MLA_NOPE
    rope_w = MLA_HEADS * MLA_ROPE
    q_nope = _dot(qn, wq_ref[:, 0:nope_w])
    q_t = _dot(qn, wq_ref[:, nope_w:nope_w + rope_w])
    q_r = _dot(qn, wq_ref[:, nope_w + rope_w:nope_w + 2 * rope_w])
    for p in range(MLA_HEADS // 2):
        q_pe = q_t[:, p * 128:(p + 1) * 128] * cos + q_r[:, p * 128:(p + 1) * 128] * sin
        for j in range(2):
            h = 2 * p + j
            q_ref[0, h, :, 0:MLA_NOPE] = (q_nope[:, h * MLA_NOPE:(h + 1) * MLA_NOPE] * qscale).astype(q_ref.dtype)
            q_ref[0, h, :, MLA_NOPE:MLA_QK] = (q_pe[:, j * MLA_ROPE:(j + 1) * MLA_ROPE] * qscale).astype(q_ref.dtype)
    kv = _dot(kvn, wkv_ref[...])
    pe = pe_ref[0]
    k_pe = (pe[:, :MLA_ROPE] * cos[:, :MLA_ROPE] + pe[:, MLA_ROPE:] * sin[:, :MLA_ROPE]).astype(k_ref.dtype)
    hw = MLA_NOPE + MLA_DV
    for h in range(MLA_HEADS):
        k_ref[0, h, :, 0:MLA_NOPE] = kv[:, h * hw:h * hw + MLA_NOPE].astype(k_ref.dtype)
        k_ref[0, h, :, MLA_NOPE:MLA_QK] = k_pe
        v_ref[0, h] = kv[:, h * hw + MLA_NOPE:(h + 1) * hw].astype(v_ref.dtype)


def _mla_prep(mq, mkv, pe, cos2, sin2, qg, kvg, wq, wkv, ts, qscale):
    b, s, _ = mq.shape
    return pl.pallas_call(
        functools.partial(_mla_prep_kernel, qscale=qscale),
        grid=(b, s // ts),
        in_specs=[pl.BlockSpec((1, ts, MLA_Q_LORA), lambda i, j: (i, j, 0)),
                  pl.BlockSpec((1, ts, MLA_KV_LORA), lambda i, j: (i, j, 0)),
                  pl.BlockSpec((1, ts, 128), lambda i, j: (i, j, 0)),
                  pl.BlockSpec((ts, 128), lambda i, j: (j, 0)),
                  pl.BlockSpec((ts, 128), lambda i, j: (j, 0)),
                  pl.BlockSpec((1, MLA_Q_LORA), lambda i, j: (0, 0)),
                  pl.BlockSpec((1, MLA_KV_LORA), lambda i, j: (0, 0)),
                  pl.BlockSpec(wq.shape, lambda i, j: (0, 0)),
                  pl.BlockSpec(wkv.shape, lambda i, j: (0, 0))],
        out_specs=[pl.BlockSpec((1, MLA_HEADS, ts, MLA_QK), lambda i, j: (i, 0, j, 0)),
                   pl.BlockSpec((1, MLA_HEADS, ts, MLA_QK), lambda i, j: (i, 0, j, 0)),
                   pl.BlockSpec((1, MLA_HEADS, ts, MLA_DV), lambda i, j: (i, 0, j, 0))],
        out_shape=[jax.ShapeDtypeStruct((b, MLA_HEADS, s, MLA_QK), BF16),
                   jax.ShapeDtypeStruct((b, MLA_HEADS, s, MLA_QK), BF16),
                   jax.ShapeDtypeStruct((b, MLA_HEADS, s, MLA_DV), BF16)],
        compiler_params=pltpu.CompilerParams(dimension_semantics=("parallel", "parallel"),
                                             vmem_limit_bytes=VMEM_LIMIT),
        name="mla_prep",
    )(mq, mkv, pe, cos2, sin2, qg, kvg, wq, wkv)


def _attn_kernel(q_ref, k_ref, v_ref, mz_ref, o_ref, s_a, s_b, p_a, p_b, m_ref, l_ref, alpha_ref, acc_ref, *, t):
    nq = q_ref.shape[2] // t
    cols = t // 128
    s_bufs = (s_a, s_b)
    p_bufs = (p_a, p_b)
    pairs = [(i, kb) for i in range(nq) for kb in range(i + 1)]

    def scores_stage(n):
        i, kb = pairs[n]
        s_bufs[n % 2][...] = lax.dot_general(q_ref[0, 0, i * t:(i + 1) * t, :], k_ref[0, 0, kb * t:(kb + 1) * t, :],
                                             (((1,), (1,)), ((), ())), preferred_element_type=F32)

    def softmax_stage(n):
        i, kb = pairs[n]
        s_buf = s_bufs[n % 2]
        s_cols = [s_buf[:, c * 128:(c + 1) * 128] for c in range(cols)]
        if kb == i:
            r = lax.broadcasted_iota(jnp.int32, (t, 128), 0)
            lane = lax.broadcasted_iota(jnp.int32, (t, 128), 1)
            s_cols = [jnp.where(lane + c * 128 <= r, s_c, -jnp.inf) for c, s_c in enumerate(s_cols)]
        col_max = s_cols[0]
        for s_c in s_cols[1:]:
            col_max = jnp.maximum(col_max, s_c)
        m_new = jnp.broadcast_to(jnp.max(col_max, axis=-1, keepdims=True), (t, 128))
        if kb > 0:
            m_old = m_ref[...]
            m_new = jnp.maximum(m_old, m_new)
            alpha = jnp.exp2(m_old - m_new)
            alpha_ref[...] = alpha
        p_cols = [jnp.exp2(s_c - m_new) for s_c in s_cols]
        col_sum = p_cols[0]
        for p_c in p_cols[1:]:
            col_sum = col_sum + p_c
        row_sum = jnp.broadcast_to(jnp.sum(col_sum, axis=-1, keepdims=True), (t, 128))
        l_ref[...] = row_sum if kb == 0 else alpha * l_ref[...] + row_sum
        m_ref[...] = m_new
        for c, p_c in enumerate(p_cols):
            p_bufs[n % 2][:, c * 128:(c + 1) * 128] = p_c.astype(p_a.dtype)

    def pv_stage(n):
        i, kb = pairs[n]
        acc = _dot(p_bufs[n % 2][...], v_ref[0, 0, kb * t:(kb + 1) * t, :])
        if kb > 0:
            acc = alpha_ref[...] * acc_ref[...] + acc
        if kb == i:
            z = mz_ref[0, i * t:(i + 1) * t, :].astype(F32)
            o_ref[0, i * t:(i + 1) * t, :] = (acc / l_ref[...] * _silu(z)).astype(o_ref.dtype)
        else:
            acc_ref[...] = acc

    scores_stage(0)
    for n in range(len(pairs)):
        if n > 0:
            pv_stage(n - 1)
        if n + 1 < len(pairs):
            scores_stage(n + 1)
        softmax_stage(n)
    pv_stage(len(pairs) - 1)


def _attention(q, k, v, mz, t):
    b, h, s, _ = q.shape
    return pl.pallas_call(
        functools.partial(_attn_kernel, t=t),
        grid=(b, h),
        in_specs=[pl.BlockSpec((1, 1, s, MLA_QK), lambda bi, hi: (bi, hi, 0, 0)),
                  pl.BlockSpec((1, 1, s, MLA_QK), lambda bi, hi: (bi, hi, 0, 0)),
                  pl.BlockSpec((1, 1, s, MLA_DV), lambda bi, hi: (bi, hi, 0, 0)),
                  pl.BlockSpec((1, s, MLA_DV), lambda bi, hi: (bi, 0, hi))],
        out_specs=pl.BlockSpec((1, s, MLA_DV), lambda bi, hi: (bi, 0, hi)),
        out_shape=jax.ShapeDtypeStruct((b, s, h * MLA_DV), BF16),
        scratch_shapes=[pltpu.VMEM((t, t), F32), pltpu.VMEM((t, t), F32),
                        pltpu.VMEM((t, t), BF16), pltpu.VMEM((t, t), BF16),
                        pltpu.VMEM((t, 128), F32), pltpu.VMEM((t, 128), F32), pltpu.VMEM((t, 128), F32),
                        pltpu.VMEM((t, MLA_DV), F32)],
        compiler_params=pltpu.CompilerParams(dimension_semantics=("parallel", "parallel"),
                                             vmem_limit_bytes=VMEM_LIMIT),
        name="attention",
    )(q, k, v, mz)


def _lru_kernel(u_ref, lz_ref, wr_ref, br_ref, wi_ref, bi_ref, ap_ref, o_ref, h_buf, hc, *, ts):
    @pl.when(pl.program_id(1) == 0)
    def _():
        hc[...] = jnp.zeros(hc.shape, F32)

    u16 = u_ref[0]
    u = u16.astype(F32)
    n_groups = LRU_WIDTH // LRU_GROUP
    r_lin = jnp.concatenate(
        [_dot(u16[:, g * LRU_GROUP:(g + 1) * LRU_GROUP], wr_ref[g]) for g in range(n_groups)], axis=1)
    i_lin = jnp.concatenate(
        [_dot(u16[:, g * LRU_GROUP:(g + 1) * LRU_GROUP], wi_ref[g]) for g in range(n_groups)], axis=1)
    r = _sigmoid(r_lin + br_ref[...])
    gate_i = _sigmoid(i_lin + bi_ref[...])
    log_a = -RG_C * r * _softplus(-ap_ref[...])
    a = jnp.exp(log_a)
    bco = jnp.sqrt(-jnp.tanh(log_a) * (a * a + 1.0)) * (gate_i * u)

    sub = 8
    rows = lax.broadcasted_iota(jnp.int32, (sub, LRU_WIDTH), 0)
    h_prev = jnp.broadcast_to(hc[0:1, :], (sub, LRU_WIDTH))
    for g in range(ts // sub):
        a_g = a[g * sub:(g + 1) * sub, :]
        b_g = bco[g * sub:(g + 1) * sub, :]
        d = 1
        while d < sub:
            keep = rows >= d
            a_s = jnp.where(keep, pltpu.roll(a_g, d, 0), 1.0)
            b_s = jnp.where(keep, pltpu.roll(b_g, d, 0), 0.0)
            b_g = a_g * b_s + b_g
            a_g = a_g * a_s
            d *= 2
        h_g = a_g * h_prev + b_g
        h_buf[g * sub:(g + 1) * sub, :] = h_g
        h_prev = jnp.broadcast_to(h_g[sub - 1:sub, :], (sub, LRU_WIDTH))
    hc[...] = h_prev
    z = lz_ref[0].astype(F32)
    o_ref[0] = (h_buf[...] * _silu(z)).astype(o_ref.dtype)


def _lru(u, lz, wr, br, wi, bi, ap, ts):
    b, s, _ = u.shape
    row = lambda i, j: (0, 0)
    return pl.pallas_call(
        functools.partial(_lru_kernel, ts=ts),
        grid=(b, s // ts),
        in_specs=[pl.BlockSpec((1, ts, LRU_WIDTH), lambda i, j: (i, j, 0)),
                  pl.BlockSpec((1, ts, LRU_WIDTH), lambda i, j: (i, j, 0)),
                  pl.BlockSpec(wr.shape, lambda i, j: (0, 0, 0)),
                  pl.BlockSpec((1, LRU_WIDTH), row),
                  pl.BlockSpec(wi.shape, lambda i, j: (0, 0, 0)),
                  pl.BlockSpec((1, LRU_WIDTH), row),
                  pl.BlockSpec((1, LRU_WIDTH), row)],
        out_specs=pl.BlockSpec((1, ts, LRU_WIDTH), lambda i, j: (i, j, 0)),
        out_shape=jax.ShapeDtypeStruct((b, s, LRU_WIDTH), BF16),
        scratch_shapes=[pltpu.VMEM((ts, LRU_WIDTH), F32),
                        pltpu.VMEM((8, LRU_WIDTH), F32)],
        compiler_params=pltpu.CompilerParams(dimension_semantics=("parallel", "arbitrary"),
                                             vmem_limit_bytes=VMEM_LIMIT),
        name="rglru",
    )(u, lz, wr, br, wi, bi, ap)


def _outproj_kernel(x_ref, oa_ref, ob_ref, oc_ref, w_ref, g_ref, y_ref, *, final):
    a_end = GDN_WIDTH
    b_end = GDN_WIDTH + MLA_WIDTH
    y = (x_ref[...] + _dot(oa_ref[...], w_ref[0:a_end, :]) + _dot(ob_ref[...], w_ref[a_end:b_end, :])
         + _dot(oc_ref[...], w_ref[b_end:MIX_WIDTH, :]))
    if final:
        y = _rms(y, g_ref[...])
    y_ref[...] = y


def _outproj(x2d, oa, ob, oc, w_out, gain, tm, final):
    t = x2d.shape[0]
    return pl.pallas_call(
        functools.partial(_outproj_kernel, final=final),
        grid=(t // tm,),
        in_specs=[pl.BlockSpec((tm, D_MODEL), lambda i: (i, 0)),
                  pl.BlockSpec((tm, GDN_WIDTH), lambda i: (i, 0)),
                  pl.BlockSpec((tm, MLA_WIDTH), lambda i: (i, 0)),
                  pl.BlockSpec((tm, LRU_WIDTH), lambda i: (i, 0)),
                  pl.BlockSpec((MIX_WIDTH, D_MODEL), lambda i: (0, 0)),
                  pl.BlockSpec((1, D_MODEL), lambda i: (0, 0))],
        out_specs=pl.BlockSpec((tm, D_MODEL), lambda i: (i, 0)),
        out_shape=jax.ShapeDtypeStruct((t, D_MODEL), F32),
        compiler_params=pltpu.CompilerParams(dimension_semantics=("parallel",),
                                             vmem_limit_bytes=VMEM_LIMIT),
        name="outproj",
    )(x2d, oa, ob, oc, w_out, gain)


def _rot_half_cols(w):
    half = MLA_ROPE // 2
    return jnp.concatenate([-w[..., half:], w[..., :half]], axis=-1)


def _pack_w_in(w):
    o = [0]
    for width in (GDN_WIDTH, GDN_WIDTH, GDN_WIDTH, GDN_HEADS, GDN_HEADS, GDN_WIDTH,
                  MLA_Q_LORA, MLA_KV_LORA, MLA_ROPE, MLA_WIDTH, LRU_WIDTH, LRU_WIDTH):
        o.append(o[-1] + width)
    w = w.astype(BF16)
    mkr = w[:, o[8]:o[9]]
    gates = jnp.pad(w[:, o[3]:o[5]], ((0, 0), (0, 128 - 2 * GDN_HEADS)))
    parts = [w[:, o[0]:o[3]], w[:, o[10]:o[11]], w[:, o[5]:o[6]], w[:, o[6]:o[7]], w[:, o[7]:o[8]],
             w[:, o[9]:o[10]], w[:, o[11]:o[12]], mkr, _rot_half_cols(mkr), gates]
    return jnp.concatenate(parts, axis=1)


def _pack_w_uq(w):
    w3 = w.reshape(MLA_Q_LORA, MLA_HEADS, MLA_QK)
    nope = w3[:, :, :MLA_NOPE].reshape(MLA_Q_LORA, MLA_HEADS * MLA_NOPE)
    rope = w3[:, :, MLA_NOPE:]
    rot = _rot_half_cols(rope)
    return jnp.concatenate([nope, rope.reshape(MLA_Q_LORA, -1), rot.reshape(MLA_Q_LORA, -1)], axis=1).astype(BF16)


def _pack_block_diag(w):
    per = LRU_GROUP // LRU_BW
    groups = LRU_BLOCKS // per
    w4 = w.astype(BF16).reshape(groups, per, LRU_BW, LRU_BW)
    on_diag = jnp.eye(per, dtype=bool)[None, :, None, :, None]
    out = jnp.where(on_diag, w4[:, :, :, None, :], jnp.zeros((), BF16))
    return out.reshape(groups, LRU_GROUP, LRU_GROUP)


def _lane_row(vals, offset):
    return jnp.pad(vals.astype(F32), (offset, 128 - offset - vals.shape[0])).reshape(1, 128)


def _rope_tables(s):
    inv = 1.0 / (ROPE_THETA ** (np.arange(0, MLA_ROPE, 2, dtype=np.float64) / MLA_ROPE))
    ang = np.arange(s, dtype=np.float64)[:, None] * inv[None, :]
    ang = np.concatenate([ang, ang, ang, ang], axis=-1)
    return jnp.asarray(np.cos(ang), F32), jnp.asarray(np.sin(ang), F32)


def kernel(x, norm_gain, w_in, gdn_conv_w, gdn_a_log, gdn_dt_bias, gdn_out_norm, mla_q_norm, mla_w_uq,
           mla_kv_norm, mla_w_ukv, lru_conv_w, lru_conv_b, lru_w_r, lru_b_r, lru_w_i, lru_b_i, lru_a_param,
           w_out, final_norm):
    b, s, d = x.shape
    depth = w_in.shape[0]
    t = b * s
    tm = min(512, t)
    ts = min(512, s)
    t_attn = min(512, s)
    ts_lru = min(1024, s)
    bb = min(4, b)
    qscale = (MLA_QK ** -0.5) * 1.4426950408889634
    cos2, sin2 = _rope_tables(s)

    h = x.reshape(t, d)
    for l in range(depth):
        conv_w = jnp.concatenate([gdn_conv_w[l], lru_conv_w[l]], axis=1)
        segs = _inproj(h, norm_gain[l].reshape(1, d), _pack_w_in(w_in[l]), conv_w, lru_conv_b[l].reshape(1, -1),
                       tm, s // tm)
        qkv, u, gz, mq, mkv, mz, lz, pe, gt = [a.reshape(b, s, a.shape[-1]) for a in segs]

        o_a = _gdn(qkv, gt, gz, _lane_row(gdn_a_log[l], GDN_HEADS), _lane_row(gdn_dt_bias[l], GDN_HEADS),
                   gdn_out_norm[l].reshape(1, GDN_DV), bb)

        q, k, v = _mla_prep(mq, mkv, pe, cos2, sin2, mla_q_norm[l].reshape(1, -1), mla_kv_norm[l].reshape(1, -1),
                            _pack_w_uq(mla_w_uq[l]), mla_w_ukv[l].astype(BF16), ts, qscale)
        o_b = _attention(q, k, v, mz, t_attn)

        o_c = _lru(u, lz, _pack_block_diag(lru_w_r[l]), lru_b_r[l].reshape(1, -1), _pack_block_diag(lru_w_i[l]),
                   lru_b_i[l].reshape(1, -1), lru_a_param[l].reshape(1, -1), ts_lru)

        h = _outproj(h, o_a.reshape(t, -1), o_b.reshape(t, -1), o_c.reshape(t, -1), w_out[l].astype(BF16),
                     final_norm.reshape(1, d), tm, final=(l == depth - 1))
    return h.reshape(b, s, d)
```
